```python
import jax, jax.numpy as jnp
from jax import lax
import numpy as np

D_MODEL = 1024
BATCH = 8
SEQ = 4096
DEPTH = 2

HG_HEADS = 4
HG_KDIM = 128
HG_VDIM = 128
HG_CHUNK = 64
ATT_GROUPS = ((128, 1), (512, 4), (2048, 16))
N_GROUPS = 3
ATT_HEADS = 4
ATT_DIM = 128
ROPE_THETA = 500000.0
ROPE_DIM = ATT_DIM // 4
D_FF = 4 * D_MODEL
EPS = 1e-6

HG_QK_W = HG_HEADS * HG_KDIM
HG_V_W = HG_HEADS * HG_VDIM
ATT_W = ATT_HEADS * ATT_DIM
ATT_QKV_W = 3 * N_GROUPS * ATT_W
SPLITS = (HG_QK_W, HG_QK_W, HG_V_W, HG_V_W, ATT_QKV_W, D_MODEL, D_MODEL)
N_IN = HG_QK_W + HG_QK_W + HG_V_W + HG_V_W + ATT_QKV_W + D_MODEL + D_MODEL

kernel_name = "hybrid_hgrn2_dilated_swa_gated_block"


def rms_norm(x, g):
    xf = x.astype(jnp.float32)
    y = xf * lax.rsqrt(jnp.mean(xf * xf, axis=-1, keepdims=True) + EPS)
    return (y * g.astype(jnp.float32)).astype(x.dtype)


def partial_rope(t, cos, sin):
    tf = t.astype(jnp.float32)
    c = cos[:, None, :]
    s = sin[:, None, :]
    half = ROPE_DIM // 2
    r1 = tf[..., :half]
    r2 = tf[..., half:ROPE_DIM]
    out = jnp.concatenate([r1 * c - r2 * s, r2 * c + r1 * s, tf[..., ROPE_DIM:]], axis=-1)
    return out.astype(t.dtype)


def hgrn2_chunked(q, log_f, k, v):
    B, S, H, K = q.shape
    V = v.shape[-1]
    C = HG_CHUNK
    nc = S // C

    def chunks(t):
        return t.reshape(B, nc, C, H, t.shape[-1]).swapaxes(0, 1)

    causal = jnp.tril(jnp.ones((C, C), dtype=bool))[None, :, :, None, None]

    def step(state, xs):
        qc, gc, kc, vc = xs
        b = jnp.cumsum(gc, axis=1)
        rel = jnp.where(causal, b[:, :, None] - b[:, None, :], -jnp.inf)
        attn = jnp.einsum('bthk,btshk,bshk->bhts', qc, jnp.exp(rel), kc)
        o = (jnp.einsum('bhts,bshv->bthv', attn, vc)
             + jnp.einsum('bthk,bhkv->bthv', qc * jnp.exp(b), state))
        b_last = b[:, -1]
        k_dec = kc * jnp.exp(b_last[:, None] - b)
        state = state * jnp.exp(b_last)[..., None] + jnp.einsum('bshk,bshv->bhkv', k_dec, vc)
        return state, o

    state0 = jnp.zeros((B, H, K, V), jnp.float32)
    _, o = lax.scan(step, state0, (chunks(q), chunks(log_f), chunks(k), chunks(v)))
    return o.swapaxes(0, 1).reshape(B, S, H, V)


def dilated_window_attention(q, k, v, window, dilation):
    B, S, H, Dh = q.shape
    L = window // dilation
    period = dilation * L
    Sp = -(-S // period) * period
    M = Sp // dilation
    nb = M // L
    pad = Sp - S

    def to_blocks(t):
        t = jnp.pad(t, ((0, 0), (0, pad), (0, 0), (0, 0)))
        t = t.reshape(B, M, dilation, H, Dh)
        t = t.transpose(0, 2, 3, 1, 4)
        return t.reshape(B, dilation, H, nb, L, Dh)

    def with_prev(t):
        prev = jnp.pad(t, ((0, 0), (0, 0), (0, 0), (1, 0), (0, 0), (0, 0)))[:, :, :, :-1]
        return jnp.concatenate([prev, t], axis=4)

    qb = to_blocks(q).astype(jnp.float32)
    kw = with_prev(to_blocks(k)).astype(jnp.float32)
    vw = with_prev(to_blocks(v)).astype(jnp.float32)

    scores = jnp.einsum('bzhnqd,bzhnkd->bzhnqk', qb, kw) * (Dh ** -0.5)
    i = jnp.arange(L)[:, None]
    j = jnp.arange(2 * L)[None, :]
    delta = L + i - j
    band = (delta >= 0) & (delta <= L)
    n = jnp.arange(nb)[:, None, None]
    mask = band[None] & ((n > 0) | (j[None] >= L))
    scores = jnp.where(mask, scores, -jnp.inf)
    m = jnp.max(scores, axis=-1, keepdims=True)
    p = jnp.exp(scores - m)
    den = jnp.sum(p, axis=-1, keepdims=True)
    o = jnp.einsum('bzhnqk,bzhnkd->bzhnqd', p, vw) / den
    lse = (m + jnp.log(den))[..., 0]

    o = o.reshape(B, dilation, H, M, Dh).transpose(0, 3, 1, 2, 4).reshape(B, Sp, H, Dh)[:, :S]
    lse = lse.reshape(B, dilation, H, M).transpose(0, 3, 1, 2).reshape(B, Sp, H)[:, :S]
    return o, lse


def setup_inputs(seed: int = 0) -> dict:
    key = jax.random.key(seed)
    ks = jax.random.split(key, 13)
    f32 = jnp.float32
    nrm = lambda k, shape, scale: jax.random.normal(k, shape, f32) * scale
    return {
        "x": nrm(ks[0], (BATCH, SEQ, D_MODEL), 1.0),
        "norm1_g": 1.0 + nrm(ks[1], (DEPTH, D_MODEL), 0.02),
        "w_in": nrm(ks[2], (DEPTH, D_MODEL, N_IN), D_MODEL ** -0.5),
        "hg_lower_bounds": nrm(ks[3], (DEPTH, HG_QK_W), 1.0),
        "hg_norm_g": 1.0 + nrm(ks[4], (DEPTH, HG_V_W), 0.02),
        "w_branch_a": nrm(ks[5], (DEPTH, HG_V_W, D_MODEL), HG_V_W ** -0.5),
        "w_branch_b": nrm(ks[6], (DEPTH, ATT_W, D_MODEL), ATT_W ** -0.5),
        "w_out": nrm(ks[7], (DEPTH, D_MODEL, D_MODEL), D_MODEL ** -0.5),
        "norm2_g": 1.0 + nrm(ks[8], (DEPTH, D_MODEL), 0.02),
        "w_up": nrm(ks[9], (DEPTH, D_MODEL, D_FF), D_MODEL ** -0.5),
        "w_down": nrm(ks[10], (DEPTH, D_FF, D_MODEL), D_FF ** -0.5),
        "final_norm_g": 1.0 + nrm(ks[11], (D_MODEL,), 0.02),
    }


def reference(x, norm1_g, w_in, hg_lower_bounds, hg_norm_g, w_branch_a, w_branch_b,
              w_out, norm2_g, w_up, w_down, final_norm_g):
    B, S, _ = x.shape
    f32 = jnp.float32
    split_idx = [int(c) for c in np.cumsum(SPLITS)[:-1]]

    pos = jnp.arange(S, dtype=f32)
    inv_freq = ROPE_THETA ** (-jnp.arange(0, ROPE_DIM, 2, dtype=f32) / ROPE_DIM)
    ang = pos[:, None] * inv_freq[None, :]
    cos, sin = jnp.cos(ang), jnp.sin(ang)

    lb_all = jnp.cumsum(jax.nn.softmax(hg_lower_bounds.astype(f32), axis=0), axis=0)
    lb_all = lb_all - lb_all[0:1]

    for l in range(DEPTH):
        h = rms_norm(x, norm1_g[l])
        z = h @ w_in[l]
        zq, zf, zi, zg, zatt, za, zb = jnp.split(z, split_idx, axis=-1)

        q_hg = jax.nn.silu(zq.astype(f32)).reshape(B, S, HG_HEADS, HG_KDIM)
        lb = lb_all[l]
        log_f = jnp.logaddexp(jnp.log(lb), jnp.log1p(-lb) + jax.nn.log_sigmoid(zf.astype(f32)))
        log_f = log_f.reshape(B, S, HG_HEADS, HG_KDIM)
        k_hg = -jnp.expm1(log_f)
        v_hg = zi.astype(f32).reshape(B, S, HG_HEADS, HG_VDIM)
        o_hg = hgrn2_chunked(q_hg, log_f, k_hg, v_hg)
        o_hg = rms_norm(o_hg, hg_norm_g[l].reshape(HG_HEADS, HG_VDIM))
        o_hg = o_hg * jax.nn.sigmoid(zg.astype(f32)).reshape(B, S, HG_HEADS, HG_VDIM)
        o_hg = o_hg.reshape(B, S, HG_V_W).astype(x.dtype)

        qkv = zatt.reshape(B, S, 3, N_GROUPS, ATT_HEADS, ATT_DIM)
        outs, lses = [], []
        for g, (window, dilation) in enumerate(ATT_GROUPS):
            qg = partial_rope(qkv[:, :, 0, g], cos, sin)
            kg = partial_rope(qkv[:, :, 1, g], cos, sin)
            vg = qkv[:, :, 2, g]
            o_g, lse_g = dilated_window_attention(qg, kg, vg, window, dilation)
            outs.append(o_g)
            lses.append(lse_g)
        w_grp = jax.nn.softmax(jnp.stack(lses, axis=0), axis=0)
        o_att = jnp.sum(w_grp[..., None] * jnp.stack(outs, axis=0), axis=0)
        o_att = o_att.reshape(B, S, ATT_W).astype(x.dtype)

        y = (jax.nn.sigmoid(za) * (o_hg @ w_branch_a[l])
             + jax.nn.sigmoid(zb) * (o_att @ w_branch_b[l]))
        x = x + y @ w_out[l]

        h2 = rms_norm(x, norm2_g[l])
        x = x + jnp.square(jax.nn.relu(h2 @ w_up[l])) @ w_down[l]

    return rms_norm(x, final_norm_g)
```

```python
import functools

import numpy as np
import jax
import jax.numpy as jnp
from jax import lax
from jax.experimental import pallas as pl
from jax.experimental.pallas import tpu as pltpu

F32 = jnp.float32
BF16 = jnp.bfloat16

D_MODEL = 1024
HG_HEADS = 4
HG_DIM = 128
HG_W = HG_HEADS * HG_DIM
ATT_GROUPS = ((128, 1), (512, 4), (2048, 16))
N_GROUPS = len(ATT_GROUPS)
ATT_HEADS = 4
ATT_DIM = 128
ATT_W = ATT_HEADS * ATT_DIM
ATT_QKV_W = 3 * N_GROUPS * ATT_W
ATT_L = 128
ROPE_THETA = 500000.0
ROPE_DIM = ATT_DIM // 4
ROPE_HALF = ROPE_DIM // 2
D_FF = 4 * D_MODEL
EPS = 1e-6
N_IN = 4 * HG_W + ATT_QKV_W + 2 * D_MODEL

COL_TILE = 512
N_COL_TILES = N_IN // COL_TILE
HG_CHUNK = 64
HG_LEVELS = 6
NEG_BIG = -1e30
VMEM_LIMIT = 56 * 1024 * 1024


def _sigmoid(z):
    return 1.0 / (1.0 + jnp.exp(-z))


def _dot(a, b):
    return jnp.dot(a, b, preferred_element_type=F32)


def _dot_nt(a, b):
    return lax.dot_general(a, b, (((1,), (1,)), ((), ())), preferred_element_type=F32)


def _dot_tn(a, b):
    return lax.dot_general(a, b, (((0,), (0,)), ((), ())), preferred_element_type=F32)


def _inproj_body(layer, x_ref, g1_ref, w_ref, lb_ref, rc_ref, ra_ref, rb_ref,
                 q_ref, lf_ref, k_ref, vg_ref, qkv_ref, gab_ref, h_scr):
    j = pl.program_id(1)

    @pl.when(j == 0)
    def _():
        x = x_ref[...]
        ms = jnp.mean(x * x, axis=-1, keepdims=True)
        h_scr[...] = (x * lax.rsqrt(ms + EPS) * g1_ref[...]).astype(BF16)

    acc = _dot(h_scr[...], w_ref[...])

    @pl.when(j == 0)
    def _():
        q_ref[...] = acc * _sigmoid(acc)

    @pl.when(j == 1)
    def _():
        raw = lb_ref[...]
        depth = raw.shape[0]
        rows = [raw[r:r + 1, :] for r in range(depth)]
        mx = functools.reduce(jnp.maximum, rows)
        ex = [jnp.exp(r - mx) for r in rows]
        den = functools.reduce(lambda a, b: a + b, ex)
        sm = [e / den for e in ex]
        cs0 = sm[0]
        csl = functools.reduce(lambda a, b: a + b, sm[:layer + 1])
        lb = csl - cs0
        log_sig = jnp.minimum(acc, 0.0) - jnp.log1p(jnp.exp(-jnp.abs(acc)))
        a = jnp.log(lb)
        c = jnp.log1p(-lb) + log_sig
        mxa = jnp.maximum(a, c)
        lf_ref[...] = mxa + jnp.log1p(jnp.exp(-jnp.abs(a - c)))
        k_ref[...] = (1.0 - lb) / (1.0 + jnp.exp(acc))

    @pl.when(j == 2)
    def _():
        vg_ref[...] = acc.astype(BF16)

    @pl.when(j == 3)
    def _():
        vg_ref[...] = _sigmoid(acc).astype(BF16)

    @pl.when(jnp.logical_and(j >= 4, j < 10))
    def _():
        rc = rc_ref[...]
        ra = ra_ref[...]
        rb = rb_ref[...]
        for h in range(ATT_HEADS):
            t = acc[:, h * ATT_DIM:(h + 1) * ATT_DIM]
            out = (t * rc + pltpu.roll(t, ROPE_HALF, 1) * ra
                   + pltpu.roll(t, ATT_DIM - ROPE_HALF, 1) * rb)
            qkv_ref[:, h * ATT_DIM:(h + 1) * ATT_DIM] = out.astype(BF16)

    @pl.when(jnp.logical_and(j >= 10, j < 13))
    def _():
        qkv_ref[...] = acc.astype(BF16)

    @pl.when(j >= 13)
    def _():
        gab_ref[...] = _sigmoid(acc).astype(BF16)


def _inproj(x2, g1, w_bf, lb_raw, rope_c, rope_a, rope_b, layer, seq, tm):
    t_rows = x2.shape[0]
    n_i = t_rows // tm
    s_tiles = seq // tm
    grid = (n_i, N_COL_TILES)

    def rope_map(i, j):
        return (jnp.where(j < 7, 0, 1), i % s_tiles, 0)

    rope_spec = pl.BlockSpec((None, tm, ATT_DIM), rope_map)
    in_specs = [
        pl.BlockSpec((tm, D_MODEL), lambda i, j: (i, 0)),
        pl.BlockSpec((1, D_MODEL), lambda i, j: (0, 0)),
        pl.BlockSpec((D_MODEL, COL_TILE), lambda i, j: (0, j)),
        pl.BlockSpec(lb_raw.shape, lambda i, j: (0, 0)),
        rope_spec, rope_spec, rope_spec,
    ]
    out_specs = [
        pl.BlockSpec((tm, COL_TILE), lambda i, j: (i, 0)),
        pl.BlockSpec((tm, COL_TILE), lambda i, j: (i, 0)),
        pl.BlockSpec((tm, COL_TILE), lambda i, j: (i, 0)),
        pl.BlockSpec((tm, COL_TILE), lambda i, j: (i, jnp.clip(j - 2, 0, 1))),
        pl.BlockSpec((tm, COL_TILE), lambda i, j: (i, jnp.clip(j - 4, 0, 8))),
        pl.BlockSpec((tm, COL_TILE), lambda i, j: (i, jnp.clip(j - 13, 0, 3))),
    ]
    out_shape = [
        jax.ShapeDtypeStruct((t_rows, HG_W), F32),
        jax.ShapeDtypeStruct((t_rows, HG_W), F32),
        jax.ShapeDtypeStruct((t_rows, HG_W), F32),
        jax.ShapeDtypeStruct((t_rows, 2 * HG_W), BF16),
        jax.ShapeDtypeStruct((t_rows, ATT_QKV_W), BF16),
        jax.ShapeDtypeStruct((t_rows, 2 * D_MODEL), BF16),
    ]
    return pl.pallas_call(
        functools.partial(_inproj_body, layer),
        grid=grid, in_specs=in_specs, out_specs=out_specs, out_shape=out_shape,
        scratch_shapes=[pltpu.VMEM((tm, D_MODEL), BF16)],
        compiler_params=pltpu.CompilerParams(
            dimension_semantics=("arbitrary", "arbitrary"), vmem_limit_bytes=VMEM_LIMIT),
        name="inproj",
    )(x2, g1, w_bf, lb_raw, rope_c, rope_a, rope_b)


def _hgrn_tables():
    c = HG_CHUNK
    slabs = []
    t = np.arange(c)[:, None]
    u = np.arange(c)[None, :]
    slabs.append((u <= t))
    masks = [np.eye(c, dtype=bool)]
    for lvl in range(HG_LEVELS):
        beta = 1 << lvl
        mid = (t // (2 * beta)) * (2 * beta) + beta - 1
        right = t > mid
        sel = np.where(right, (u > mid) & (u <= t), (u > t) & (u <= mid))
        slabs.append(sel)
        s = np.arange(c)[None, :]
        same = (t // (2 * beta)) == (s // (2 * beta))
        s_left = (s % (2 * beta)) < beta
        masks.append(same & right & s_left)
    slabs.append(u > t)
    m = np.concatenate(slabs, axis=0).astype(np.float32)
    m3 = np.concatenate([m, m, m], axis=1)
    return jnp.asarray(m3, BF16), jnp.asarray(np.stack(masks).astype(np.float32))


def _hgrn_body(n_chunks, q_ref, lf_ref, k_ref, vg_ref, sel_ref, mask_ref, gn_ref, o_ref, st_ref):
    c = HG_CHUNK

    @pl.when(pl.program_id(1) == 0)
    def _():
        st_ref[...] = jnp.zeros_like(st_ref)

    row_right = []
    rid = lax.broadcasted_iota(jnp.int32, (c, 1), 0)
    for lvl in range(HG_LEVELS):
        row_right.append(((rid >> lvl) & 1) == 1)

    def chunk(ci, carry):
        r0 = pl.multiple_of(ci * c, c)
        rows = pl.ds(r0, c)
        g = lf_ref[rows, :]
        g_hi = g.astype(BF16)
        r1 = g - g_hi.astype(F32)
        g_mid = r1.astype(BF16)
        g_lo = (r1 - g_mid.astype(F32)).astype(BF16)
        g3 = jnp.concatenate([g_hi, g_mid, g_lo], axis=0)
        e_all = jnp.exp(_dot(sel_ref[...], g3))
        q = q_ref[rows, :]
        k = k_ref[rows, :]
        for h in range(HG_HEADS):
            hs = slice(h * HG_DIM, (h + 1) * HG_DIM)
            qh = q[:, hs]
            kh = k[:, hs]
            vh = vg_ref[rows, hs]
            a = mask_ref[0] * _dot_nt(qh.astype(BF16), kh.astype(BF16))
            for lvl in range(HG_LEVELS):
                e = e_all[(lvl + 1) * c:(lvl + 2) * c, hs]
                ql = jnp.where(row_right[lvl], qh * e, 0.0).astype(BF16)
                kl = jnp.where(row_right[lvl], 0.0, kh * e).astype(BF16)
                a = a + mask_ref[lvl + 1] * _dot_nt(ql, kl)
            e_b = e_all[0:c, hs]
            e_last = e_all[(HG_LEVELS + 1) * c:(HG_LEVELS + 2) * c, hs]
            st = st_ref[h]
            o = _dot(a.astype(BF16), vh) + _dot_nt((qh * e_b).astype(BF16), st.astype(BF16))
            st_ref[h] = st * e_b[c - 1:c, :] + _dot_tn(vh, (kh * e_last).astype(BF16))
            ms = jnp.mean(o * o, axis=-1, keepdims=True)
            gate = vg_ref[rows, HG_W + h * HG_DIM:HG_W + (h + 1) * HG_DIM].astype(F32)
            o_ref[rows, hs] = (o * lax.rsqrt(ms + EPS) * gn_ref[:, hs] * gate).astype(BF16)
        return carry

    lax.fori_loop(0, n_chunks, chunk, 0)


def _hgrn(q, lf, k, vg, sel, masks, gn, batch, seq, rows):
    t_rows = q.shape[0]
    n_blk = seq // rows
    grid = (batch, n_blk)
    row_map = lambda b, n: (b * n_blk + n, 0)
    in_specs = [
        pl.BlockSpec((rows, HG_W), row_map),
        pl.BlockSpec((rows, HG_W), row_map),
        pl.BlockSpec((rows, HG_W), row_map),
        pl.BlockSpec((rows, 2 * HG_W), row_map),
        pl.BlockSpec(sel.shape, lambda b, n: (0, 0)),
        pl.BlockSpec(masks.shape, lambda b, n: (0, 0, 0)),
        pl.BlockSpec((1, HG_W), lambda b, n: (0, 0)),
    ]
    return pl.pallas_call(
        functools.partial(_hgrn_body, rows // HG_CHUNK),
        grid=grid, in_specs=in_specs,
        out_specs=pl.BlockSpec((rows, HG_W), row_map),
        out_shape=jax.ShapeDtypeStruct((t_rows, HG_W), BF16),
        scratch_shapes=[pltpu.VMEM((HG_HEADS, HG_DIM, HG_DIM), F32)],
        compiler_params=pltpu.CompilerParams(
            dimension_semantics=("arbitrary", "arbitrary"), vmem_limit_bytes=VMEM_LIMIT),
        name="hgrn2",
    )(q, lf, k, vg, sel, masks, gn)


def _attn_body(q_ref, kc_ref, kp_ref, vc_ref, vp_ref, o_ref, lse_ref):
    n = pl.program_id(2)
    blk = ATT_L
    i = lax.broadcasted_iota(jnp.int32, (blk, 2 * blk), 0)
    j = lax.broadcasted_iota(jnp.int32, (blk, 2 * blk), 1)
    valid = (j >= i) & (j <= i + blk) & ((n > 0) | (j >= blk))
    lane = lax.broadcasted_iota(jnp.int32, (blk, ATT_DIM), 1)
    lse_tile = jnp.zeros((blk, ATT_DIM), F32)
    for h in range(ATT_HEADS):
        hs = slice(h * ATT_DIM, (h + 1) * ATT_DIM)
        q = q_ref[:, hs]
        kk = jnp.concatenate([kp_ref[:, hs], kc_ref[:, hs]], axis=0)
        vv = jnp.concatenate([vp_ref[:, hs], vc_ref[:, hs]], axis=0)
        s = jnp.where(valid, _dot_nt(q, kk), NEG_BIG)
        m = jnp.max(s, axis=-1, keepdims=True)
        p = jnp.exp(s - m)
        den = jnp.sum(p, axis=-1, keepdims=True)
        o = _dot(p.astype(BF16), vv) / den
        o_ref[:, hs] = o.astype(o_ref.dtype)
        lse_tile = jnp.where(lane == h, m + jnp.log(den), lse_tile)
    lse_ref[...] = lse_tile


def _attn_group(qkv, grp, batch, seq):
    _, dil = ATT_GROUPS[grp]
    m_rows = seq // dil
    nb = m_rows // ATT_L
    n_ct = ATT_QKV_W // ATT_W
    qkv_v = qkv.reshape(batch, m_rows, dil * ATT_QKV_W)
    grid = (batch, dil, nb)
    blk = (None, ATT_L, ATT_W)
    in_specs = [
        pl.BlockSpec(blk, lambda b, r, n: (b, n, r * n_ct + grp)),
        pl.BlockSpec(blk, lambda b, r, n: (b, n, r * n_ct + N_GROUPS + grp)),
        pl.BlockSpec(blk, lambda b, r, n: (b, jnp.maximum(n - 1, 0), r * n_ct + N_GROUPS + grp)),
        pl.BlockSpec(blk, lambda b, r, n: (b, n, r * n_ct + 2 * N_GROUPS + grp)),
        pl.BlockSpec(blk, lambda b, r, n: (b, jnp.maximum(n - 1, 0), r * n_ct + 2 * N_GROUPS + grp)),
    ]
    out_specs = [
        pl.BlockSpec(blk, lambda b, r, n: (b, n, r)),
        pl.BlockSpec((None, ATT_L, ATT_DIM), lambda b, r, n: (b, n, r)),
    ]
    out_shape = [
        jax.ShapeDtypeStruct((batch, m_rows, dil * ATT_W), BF16),
        jax.ShapeDtypeStruct((batch, m_rows, dil * ATT_DIM), F32),
    ]
    o, lse = pl.pallas_call(
        _attn_body, grid=grid, in_specs=in_specs, out_specs=out_specs, out_shape=out_shape,
        compiler_params=pltpu.CompilerParams(
            dimension_semantics=("arbitrary", "arbitrary", "arbitrary"),
            vmem_limit_bytes=VMEM_LIMIT),
        name=f"attn_g{grp}",
    )(qkv_v, qkv_v, qkv_v, qkv_v, qkv_v)
    return o.reshape(batch * seq, ATT_W), lse.reshape(batch * seq, ATT_DIM)


def _merge_body(x_ref, ohg_ref, o0_ref, o1_ref, o2_ref, l0_ref, l1_ref, l2_ref, gab_ref,
                exp_ref, wa_ref, wb_ref, wo_ref, out_ref):
    lses = [l0_ref[...], l1_ref[...], l2_ref[...]]
    mx = jnp.maximum(jnp.maximum(lses[0], lses[1]), lses[2])
    es = [jnp.exp(l - mx) for l in lses]
    inv = 1.0 / (es[0] + es[1] + es[2])
    o_att = None
    for e, o_ref in zip(es, (o0_ref, o1_ref, o2_ref)):
        w = e * inv
        w_hi = w.astype(BF16)
        w_lo = (w - w_hi.astype(F32)).astype(BF16)
        w_full = _dot(jnp.concatenate([w_hi, w_lo], axis=1), exp_ref[...])
        term = w_full * o_ref[...].astype(F32)
        o_att = term if o_att is None else o_att + term
    ga = gab_ref[:, :D_MODEL].astype(F32)
    gb = gab_ref[:, D_MODEL:].astype(F32)
    y = ga * _dot(ohg_ref[...], wa_ref[...]) + gb * _dot(o_att.astype(BF16), wb_ref[...])
    out_ref[...] = x_ref[...] + _dot(y.astype(BF16), wo_ref[...])


def _merge(x2, ohg, outs, lses, gab, expand, wa, wb, wo, tm):
    t_rows = x2.shape[0]
    grid = (t_rows // tm,)
    row = lambda w: pl.BlockSpec((tm, w), lambda i: (i, 0))
    full = lambda a: pl.BlockSpec(a.shape, lambda i: (0, 0))
    in_specs = [row(D_MODEL), row(HG_W), row(ATT_W), row(ATT_W), row(ATT_W),
                row(ATT_DIM), row(ATT_DIM), row(ATT_DIM), row(2 * D_MODEL),
                full(expand), full(wa), full(wb), full(wo)]
    return pl.pallas_call(
        _merge_body, grid=grid, in_specs=in_specs, out_specs=row(D_MODEL),
        out_shape=jax.ShapeDtypeStruct((t_rows, D_MODEL), F32),
        compiler_params=pltpu.CompilerParams(
            dimension_semantics=("arbitrary",), vmem_limit_bytes=VMEM_LIMIT),
        name="merge",
    )(x2, ohg, outs[0], outs[1], outs[2], lses[0], lses[1], lses[2], gab, expand, wa, wb, wo)


def _mlp_body(final, ff_tile, x_ref, g2_ref, wu_ref, wd_ref, gf_ref, out_ref):
    x = x_ref[...]
    ms = jnp.mean(x * x, axis=-1, keepdims=True)
    h = (x * lax.rsqrt(ms + EPS) * g2_ref[...]).astype(BF16)
    acc = x
    for c in range(D_FF // ff_tile):
        cs = slice(c * ff_tile, (c + 1) * ff_tile)
        u = jnp.maximum(_dot(h, wu_ref[:, cs]), 0.0)
        acc = acc + _dot((u * u).astype(BF16), wd_ref[cs, :])
    if final:
        ms2 = jnp.mean(acc * acc, axis=-1, keepdims=True)
        acc = acc * lax.rsqrt(ms2 + EPS) * gf_ref[...]
    out_ref[...] = acc


def _mlp(x2, g2, wu, wd, gf, final, tm):
    t_rows = x2.shape[0]
    grid = (t_rows // tm,)
    row = pl.BlockSpec((tm, D_MODEL), lambda i: (i, 0))
    vec = pl.BlockSpec((1, D_MODEL), lambda i: (0, 0))
    full = lambda a: pl.BlockSpec(a.shape, lambda i: (0, 0))
    return pl.pallas_call(
        functools.partial(_mlp_body, final, 1024),
        grid=grid, in_specs=[row, vec, full(wu), full(wd), vec], out_specs=row,
        out_shape=jax.ShapeDtypeStruct((t_rows, D_MODEL), F32),
        compiler_params=pltpu.CompilerParams(
            dimension_semantics=("arbitrary",), vmem_limit_bytes=VMEM_LIMIT),
        name="mlp",
    )(x2, g2, wu, wd, gf)


def _rope_tables(seq):
    pos = jnp.arange(seq, dtype=F32)
    inv_freq = ROPE_THETA ** (-jnp.arange(0, ROPE_DIM, 2, dtype=F32) / ROPE_DIM)
    ang = pos[:, None] * inv_freq[None, :]
    cos, sin = jnp.cos(ang), jnp.sin(ang)
    ones = jnp.ones((seq, ATT_DIM - ROPE_DIM), F32)
    zeros_h = jnp.zeros((seq, ROPE_HALF), F32)
    zeros_r = jnp.zeros((seq, ATT_DIM - ROPE_DIM), F32)
    c = jnp.concatenate([cos, cos, ones], axis=1)
    a = jnp.concatenate([zeros_h, sin, zeros_r], axis=1)
    b = jnp.concatenate([-sin, zeros_h, zeros_r], axis=1)
    scale = ATT_DIM ** -0.5
    stack = lambda t: jnp.stack([t * scale, t], axis=0)
    return stack(c), stack(a), stack(b)


def kernel(x, norm1_g, w_in, hg_lower_bounds, hg_norm_g, w_branch_a, w_branch_b, w_out, norm2_g,
           w_up, w_down, final_norm_g):
    batch, seq, d = x.shape
    depth = w_in.shape[0]
    t_rows = batch * seq
    tm = 512
    rope_c, rope_a, rope_b = _rope_tables(seq)
    sel, masks = _hgrn_tables()
    head_id = np.arange(ATT_DIM)[:, None]
    col_head = np.arange(ATT_W)[None, :] // ATT_DIM
    expand1 = (head_id == col_head).astype(np.float32)
    expand = jnp.asarray(np.concatenate([expand1, expand1], axis=0), BF16)
    lb_raw = hg_lower_bounds.astype(F32)
    gf = final_norm_g.reshape(1, d).astype(F32)

    x2 = x.reshape(t_rows, d)
    for l in range(depth):
        q, lf, k, vg, qkv, gab = _inproj(
            x2, norm1_g[l].reshape(1, d), w_in[l].astype(BF16), lb_raw, rope_c, rope_a, rope_b,
            l, seq, tm)
        ohg = _hgrn(q, lf, k, vg, sel, masks, hg_norm_g[l].reshape(1, HG_W), batch, seq, 512)
        outs, lses = [], []
        for grp in range(N_GROUPS):
            o_g, lse_g = _attn_group(qkv, grp, batch, seq)
            outs.append(o_g)
            lses.append(lse_g)
        x2 = _merge(x2, ohg, outs, lses, gab, expand, w_branch_a[l].astype(BF16),
                    w_branch_b[l].astype(BF16), w_out[l].astype(BF16), tm)
        x2 = _mlp(x2, norm2_g[l].reshape(1, d), w_up[l].astype(BF16), w_down[l].astype(BF16), gf,
                  l == depth - 1, tm)
    return x2.reshape(batch, seq, d)
```

```python
import functools

import numpy as np
import jax
import jax.numpy as jnp
from jax import lax
from jax.experimental import pallas as pl
from jax.experimental.pallas import tpu as pltpu

F32 = jnp.float32
BF16 = jnp.bfloat16

D_MODEL = 1024
HG_HEADS = 4
HG_DIM = 128
HG_W = HG_HEADS * HG_DIM
ATT_GROUPS = ((128, 1), (512, 4), (2048, 16))
N_GROUPS = len(ATT_GROUPS)
ATT_HEADS = 4
ATT_DIM = 128
ATT_W = ATT_HEADS * ATT_DIM
ATT_QKV_W = 3 * N_GROUPS * ATT_W
ATT_L = 128
ROPE_THETA = 500000.0
ROPE_DIM = ATT_DIM // 4
ROPE_HALF = ROPE_DIM // 2
D_FF = 4 * D_MODEL
EPS = 1e-6
N_IN = 4 * HG_W + ATT_QKV_W + 2 * D_MODEL

COL_TILE = 512
HG_CHUNK = 64
HG_LEVELS = 6
NEG_BIG = -1e30
ATT_BLOCKS_PER_STEP = (8, 2, 1)
VMEM_LIMIT = 56 * 1024 * 1024


def _sigmoid(z):
    return 0.5 * jnp.tanh(0.5 * z) + 0.5


def _dot(a, b):
    return jnp.dot(a, b, preferred_element_type=F32)


def _dot_nt(a, b):
    return lax.dot_general(a, b, (((1,), (1,)), ((), ())), preferred_element_type=F32)


def _dot_tn(a, b):
    return lax.dot_general(a, b, (((0,), (0,)), ((), ())), preferred_element_type=F32)


def _inproj_body(layer, tm, x_ref, g1_ref, w_ref, lb_ref, rc_ref, ra_ref, rb_ref,
                 q_ref, lf_ref, k_ref, vg_ref, a0_ref, a1_ref, a2_ref, gab_ref, scr_ref):
    x = x_ref[...]
    ms = jnp.mean(x * x, axis=-1, keepdims=True)
    h = (x * lax.rsqrt(ms + EPS) * g1_ref[...]).astype(BF16)

    def proj(tile):
        return _dot(h, w_ref[:, tile * COL_TILE:(tile + 1) * COL_TILE])

    acc = proj(0)
    q_ref[...] = acc * _sigmoid(acc)

    acc = proj(1)
    raw = lb_ref[...]
    rows = [raw[r:r + 1, :] for r in range(raw.shape[0])]
    mx = functools.reduce(jnp.maximum, rows)
    ex = [jnp.exp(r - mx) for r in rows]
    den = functools.reduce(lambda a, b: a + b, ex)
    sm = [e / den for e in ex]
    lb = functools.reduce(lambda a, b: a + b, sm[:layer + 1]) - sm[0]
    log_sig = jnp.minimum(acc, 0.0) - jnp.log1p(jnp.exp(-jnp.abs(acc)))
    a = jnp.log(lb)
    c = jnp.log1p(-lb) + log_sig
    lf_ref[...] = jnp.maximum(a, c) + jnp.log1p(jnp.exp(-jnp.abs(a - c)))
    k_ref[...] = (1.0 - lb) / (1.0 + jnp.exp(acc))

    vg_ref[:, :HG_W] = proj(2).astype(BF16)
    vg_ref[:, HG_W:] = _sigmoid(proj(3)).astype(BF16)

    att_refs = (a0_ref, a1_ref, a2_ref)
    slot = 0
    for grp in range(N_GROUPS):
        dil = ATT_GROUPS[grp][1]
        for which in range(3):
            acc = proj(4 + which * N_GROUPS + grp)
            if which < 2:
                pieces = []
                for hd in range(ATT_HEADS):
                    t = acc[:, hd * ATT_DIM:(hd + 1) * ATT_DIM]
                    pieces.append(t * rc_ref[which] + pltpu.roll(t, ROPE_HALF, 1) * ra_ref[which]
                                  + pltpu.roll(t, ATT_DIM - ROPE_HALF, 1) * rb_ref[which])
            else:
                pieces = [acc[:, hd * ATT_DIM:(hd + 1) * ATT_DIM] for hd in range(ATT_HEADS)]
            if dil == 1:
                for hd in range(ATT_HEADS):
                    c0 = which * ATT_W + hd * ATT_DIM
                    att_refs[grp][0, :, c0:c0 + ATT_DIM] = pieces[hd].astype(BF16)
            else:
                for hd in range(ATT_HEADS):
                    scr_ref[slot, hd] = pieces[hd]
                for r in range(dil):
                    for hd in range(ATT_HEADS):
                        c0 = which * ATT_W + hd * ATT_DIM
                        att_refs[grp][r, :, c0:c0 + ATT_DIM] = scr_ref[
                            slot, hd, pl.ds(r, tm // dil, stride=dil), :].astype(BF16)
                slot += 1

    for t in range(4):
        gab_ref[:, t * COL_TILE:(t + 1) * COL_TILE] = _sigmoid(proj(13 + t)).astype(BF16)


def _inproj(x2, g1, w_bf, lb_raw, rope_c, rope_a, rope_b, layer, batch, seq, tm):
    t_rows = x2.shape[0]
    s_tiles = seq // tm
    grid = (t_rows // tm,)
    rope_spec = pl.BlockSpec((2, tm, ATT_DIM), lambda i: (0, i % s_tiles, 0))
    in_specs = [
        pl.BlockSpec((tm, D_MODEL), lambda i: (i, 0)),
        pl.BlockSpec((1, D_MODEL), lambda i: (0, 0)),
        pl.BlockSpec((D_MODEL, N_IN), lambda i: (0, 0), pipeline_mode=pl.Buffered(1)),
        pl.BlockSpec(lb_raw.shape, lambda i: (0, 0)),
        rope_spec, rope_spec, rope_spec,
    ]
    row = lambda w: pl.BlockSpec((tm, w), lambda i: (i, 0))
    att_spec = lambda dil: pl.BlockSpec(
        (None, dil, tm // dil, 3 * ATT_W), lambda i: (i // s_tiles, 0, i % s_tiles, 0))
    out_specs = [row(HG_W), row(HG_W), row(HG_W), row(2 * HG_W)]
    out_specs += [att_spec(dil) for _, dil in ATT_GROUPS]
    out_specs += [row(2 * D_MODEL)]
    out_shape = [
        jax.ShapeDtypeStruct((t_rows, HG_W), F32),
        jax.ShapeDtypeStruct((t_rows, HG_W), F32),
        jax.ShapeDtypeStruct((t_rows, HG_W), F32),
        jax.ShapeDtypeStruct((t_rows, 2 * HG_W), BF16),
    ]
    out_shape += [jax.ShapeDtypeStruct((batch, dil, seq // dil, 3 * ATT_W), BF16)
                  for _, dil in ATT_GROUPS]
    out_shape += [jax.ShapeDtypeStruct((t_rows, 2 * D_MODEL), BF16)]
    n_slots = 3 * sum(1 for _, dil in ATT_GROUPS if dil > 1)
    return pl.pallas_call(
        functools.partial(_inproj_body, layer, tm),
        grid=grid, in_specs=in_specs, out_specs=out_specs, out_shape=out_shape,
        scratch_shapes=[pltpu.VMEM((n_slots, ATT_HEADS, tm, ATT_DIM), F32)],
        compiler_params=pltpu.CompilerParams(
            dimension_semantics=("arbitrary",), vmem_limit_bytes=VMEM_LIMIT),
        name="inproj",
    )(x2, g1, w_bf, lb_raw, rope_c, rope_a, rope_b)


def _hgrn_tables():
    c = HG_CHUNK
    slabs = []
    t = np.arange(c)[:, None]
    u = np.arange(c)[None, :]
    slabs.append((u <= t))
    masks = [np.eye(c, dtype=bool)]
    for lvl in range(HG_LEVELS):
        beta = 1 << lvl
        mid = (t // (2 * beta)) * (2 * beta) + beta - 1
        right = t > mid
        sel = np.where(right, (u > mid) & (u <= t), (u > t) & (u <= mid))
        slabs.append(sel)
        s = np.arange(c)[None, :]
        same = (t // (2 * beta)) == (s // (2 * beta))
        s_left = (s % (2 * beta)) < beta
        masks.append(same & right & s_left)
    slabs.append(u > t)
    m = np.concatenate(slabs, axis=0).astype(np.float32)
    m3 = np.concatenate([m, m, m], axis=1)
    return jnp.asarray(m3, BF16), jnp.asarray(np.stack(masks).astype(np.float32))


def _hgrn_body(n_chunks, q_ref, lf_ref, k_ref, vg_ref, sel_ref, mask_ref, gn_ref, o_ref, st_ref):
    c = HG_CHUNK

    @pl.when(pl.program_id(1) == 0)
    def _():
        st_ref[...] = jnp.zeros_like(st_ref)

    row_right = []
    rid = lax.broadcasted_iota(jnp.int32, (c, 1), 0)
    for lvl in range(HG_LEVELS):
        row_right.append(((rid >> lvl) & 1) == 1)

    def chunk(ci, carry):
        r0 = pl.multiple_of(ci * c, c)
        rows = pl.ds(r0, c)
        g = lf_ref[rows, :]
        g_hi = g.astype(BF16)
        r1 = g - g_hi.astype(F32)
        g_mid = r1.astype(BF16)
        g_lo = (r1 - g_mid.astype(F32)).astype(BF16)
        g3 = jnp.concatenate([g_hi, g_mid, g_lo], axis=0)
        e_all = jnp.exp(_dot(sel_ref[...], g3))
        q = q_ref[rows, :]
        k = k_ref[rows, :]
        for h in range(HG_HEADS):
            hs = slice(h * HG_DIM, (h + 1) * HG_DIM)
            qh = q[:, hs]
            kh = k[:, hs]
            vh = vg_ref[rows, hs]
            a = mask_ref[0] * _dot_nt(qh.astype(BF16), kh.astype(BF16))
            for lvl in range(HG_LEVELS):
                e = e_all[(lvl + 1) * c:(lvl + 2) * c, hs]
                ql = jnp.where(row_right[lvl], qh * e, 0.0).astype(BF16)
                kl = jnp.where(row_right[lvl], 0.0, kh * e).astype(BF16)
                a = a + mask_ref[lvl + 1] * _dot_nt(ql, kl)
            e_b = e_all[0:c, hs]
            e_last = e_all[(HG_LEVELS + 1) * c:(HG_LEVELS + 2) * c, hs]
            st = st_ref[h]
            o = _dot(a.astype(BF16), vh) + _dot_nt((qh * e_b).astype(BF16), st.astype(BF16))
            st_ref[h] = st * e_b[c - 1:c, :] + _dot_tn(vh, (kh * e_last).astype(BF16))
            ms = jnp.mean(o * o, axis=-1, keepdims=True)
            gate = vg_ref[rows, HG_W + h * HG_DIM:HG_W + (h + 1) * HG_DIM].astype(F32)
            o_ref[rows, hs] = (o * lax.rsqrt(ms + EPS) * gn_ref[:, hs] * gate).astype(BF16)
        return carry

    lax.fori_loop(0, n_chunks, chunk, 0)


def _hgrn(q, lf, k, vg, sel, masks, gn, batch, seq, rows):
    t_rows = q.shape[0]
    n_blk = seq // rows
    grid = (batch, n_blk)
    row_map = lambda b, n: (b * n_blk + n, 0)
    in_specs = [
        pl.BlockSpec((rows, HG_W), row_map),
        pl.BlockSpec((rows, HG_W), row_map),
        pl.BlockSpec((rows, HG_W), row_map),
        pl.BlockSpec((rows, 2 * HG_W), row_map),
        pl.BlockSpec(sel.shape, lambda b, n: (0, 0)),
        pl.BlockSpec(masks.shape, lambda b, n: (0, 0, 0)),
        pl.BlockSpec((1, HG_W), lambda b, n: (0, 0)),
    ]
    return pl.pallas_call(
        functools.partial(_hgrn_body, rows // HG_CHUNK),
        grid=grid, in_specs=in_specs,
        out_specs=pl.BlockSpec((rows, HG_W), row_map),
        out_shape=jax.ShapeDtypeStruct((t_rows, HG_W), BF16),
        scratch_shapes=[pltpu.VMEM((HG_HEADS, HG_DIM, HG_DIM), F32)],
        compiler_params=pltpu.CompilerParams(
            dimension_semantics=("arbitrary", "arbitrary"), vmem_limit_bytes=VMEM_LIMIT),
        name="hgrn2",
    )(q, lf, k, vg, sel, masks, gn)


def _attn_body(dil, n_blk, q_ref, kc_ref, kp_ref, vc_ref, vp_ref, o_ref, lse_ref, o_scr):
    n = pl.program_id(1)
    blk = ATT_L
    i = lax.broadcasted_iota(jnp.int32, (blk, 2 * blk), 0)
    j = lax.broadcasted_iota(jnp.int32, (blk, 2 * blk), 1)
    band = (j >= i) & (j <= i + blk)
    band_first = band & ((n > 0) | (j >= blk))
    lane = lax.broadcasted_iota(jnp.int32, (blk, ATT_DIM), 1)

    def residue(r, carry):
        for jb in range(n_blk):
            cur = slice(jb * blk, (jb + 1) * blk)
            prev = slice((jb - 1) * blk, jb * blk)
            valid = band_first if jb == 0 else band
            lse_tile = jnp.zeros((blk, ATT_DIM), F32)
            out_rows = pl.ds(jb * blk * dil + r, blk, stride=dil)
            for h in range(ATT_HEADS):
                hs = slice(h * ATT_DIM, (h + 1) * ATT_DIM)
                q = q_ref[r, cur, hs]
                if jb == 0:
                    k_prev, v_prev = kp_ref[r, :, hs], vp_ref[r, :, hs]
                else:
                    k_prev, v_prev = kc_ref[r, prev, hs], vc_ref[r, prev, hs]
                kk = jnp.concatenate([k_prev, kc_ref[r, cur, hs]], axis=0)
                vv = jnp.concatenate([v_prev, vc_ref[r, cur, hs]], axis=0)
                s = jnp.where(valid, _dot_nt(q, kk), NEG_BIG)
                m = jnp.max(s, axis=-1, keepdims=True)
                p = jnp.exp(s - m)
                den = jnp.sum(p, axis=-1, keepdims=True)
                o_scr[h, out_rows, :] = _dot(p.astype(BF16), vv) / den
                lse_tile = jnp.where(lane == h, m + jnp.log(den), lse_tile)
            lse_ref[out_rows, :] = lse_tile
        return carry

    if dil == 1:
        residue(0, 0)
    else:
        lax.fori_loop(0, dil, residue, 0)
    for h in range(ATT_HEADS):
        o_ref[:, h * ATT_DIM:(h + 1) * ATT_DIM] = o_scr[h].astype(o_ref.dtype)


def _attn_group(qkv_g, grp, batch, seq, n_blk):
    _, dil = ATT_GROUPS[grp]
    m_rows = seq // dil
    steps = m_rows // (ATT_L * n_blk)
    span = ATT_L * n_blk * dil
    cur = lambda c: pl.BlockSpec((None, dil, ATT_L * n_blk, ATT_W), lambda b, n: (b, 0, n, c))
    prev = lambda c: pl.BlockSpec(
        (None, dil, ATT_L, ATT_W), lambda b, n: (b, 0, jnp.maximum(n * n_blk - 1, 0), c))
    in_specs = [cur(0), cur(1), prev(1), cur(2), prev(2)]
    out_specs = [
        pl.BlockSpec((None, span, ATT_W), lambda b, n: (b, n, 0)),
        pl.BlockSpec((None, span, ATT_DIM), lambda b, n: (b, n, 0)),
    ]
    out_shape = [
        jax.ShapeDtypeStruct((batch, seq, ATT_W), BF16),
        jax.ShapeDtypeStruct((batch, seq, ATT_DIM), F32),
    ]
    o, lse = pl.pallas_call(
        functools.partial(_attn_body, dil, n_blk),
        grid=(batch, steps), in_specs=in_specs, out_specs=out_specs, out_shape=out_shape,
        scratch_shapes=[pltpu.VMEM((ATT_HEADS, span, ATT_DIM), F32)],
        compiler_params=pltpu.CompilerParams(
            dimension_semantics=("arbitrary", "arbitrary"), vmem_limit_bytes=VMEM_LIMIT),
        name=f"attn_g{grp}",
    )(qkv_g, qkv_g, qkv_g, qkv_g, qkv_g)
    return o.reshape(batch * seq, ATT_W), lse.reshape(batch * seq, ATT_DIM)


def _merge_body(x_ref, ohg_ref, o0_ref, o1_ref, o2_ref, l0_ref, l1_ref, l2_ref, gab_ref,
                exp_ref, wa_ref, wb_ref, wo_ref, out_ref):
    lses = [l0_ref[...], l1_ref[...], l2_ref[...]]
    mx = jnp.maximum(jnp.maximum(lses[0], lses[1]), lses[2])
    es = [jnp.exp(l - mx) for l in lses]
    inv = 1.0 / (es[0] + es[1] + es[2])
    o_att = None
    for e, o_ref in zip(es, (o0_ref, o1_ref, o2_ref)):
        w = e * inv
        w_hi = w.astype(BF16)
        w_lo = (w - w_hi.astype(F32)).astype(BF16)
        w_full = _dot(jnp.concatenate([w_hi, w_lo], axis=1), exp_ref[...])
        term = w_full * o_ref[...].astype(F32)
        o_att = term if o_att is None else o_att + term
    ga = gab_ref[:, :D_MODEL].astype(F32)
    gb = gab_ref[:, D_MODEL:].astype(F32)
    y = ga * _dot(ohg_ref[...], wa_ref[...]) + gb * _dot(o_att.astype(BF16), wb_ref[...])
    out_ref[...] = x_ref[...] + _dot(y.astype(BF16), wo_ref[...])


def _merge(x2, ohg, outs, lses, gab, expand, wa, wb, wo, tm):
    t_rows = x2.shape[0]
    grid = (t_rows // tm,)
    row = lambda w: pl.BlockSpec((tm, w), lambda i: (i, 0))
    full = lambda a: pl.BlockSpec(a.shape, lambda i: (0, 0))
    in_specs = [row(D_MODEL), row(HG_W), row(ATT_W), row(ATT_W), row(ATT_W),
                row(ATT_DIM), row(ATT_DIM), row(ATT_DIM), row(2 * D_MODEL),
                full(expand), full(wa), full(wb), full(wo)]
    return pl.pallas_call(
        _merge_body, grid=grid, in_specs=in_specs, out_specs=row(D_MODEL),
        out_shape=jax.ShapeDtypeStruct((t_rows, D_MODEL), F32),
        compiler_params=pltpu.CompilerParams(
            dimension_semantics=("arbitrary",), vmem_limit_bytes=VMEM_LIMIT),
        name="merge",
    )(x2, ohg, outs[0], outs[1], outs[2], lses[0], lses[1], lses[2], gab, expand, wa, wb, wo)


def _mlp_body(final, ff_tile, x_ref, g2_ref, wu_ref, wd_ref, gf_ref, out_ref):
    x = x_ref[...]
    ms = jnp.mean(x * x, axis=-1, keepdims=True)
    h = (x * lax.rsqrt(ms + EPS) * g2_ref[...]).astype(BF16)
    acc = x
    for c in range(D_FF // ff_tile):
        cs = slice(c * ff_tile, (c + 1) * ff_tile)
        u = jnp.maximum(_dot(h, wu_ref[:, cs]), 0.0)
        acc = acc + _dot((u * u).astype(BF16), wd_ref[cs, :])
    if final:
        ms2 = jnp.mean(acc * acc, axis=-1, keepdims=True)
        acc = acc * lax.rsqrt(ms2 + EPS) * gf_ref[...]
    out_ref[...] = acc


def _mlp(x2, g2, wu, wd, gf, final, tm):
    t_rows = x2.shape[0]
    grid = (t_rows // tm,)
    row = pl.BlockSpec((tm, D_MODEL), lambda i: (i, 0))
    vec = pl.BlockSpec((1, D_MODEL), lambda i: (0, 0))
    full = lambda a: pl.BlockSpec(a.shape, lambda i: (0, 0))
    return pl.pallas_call(
        functools.partial(_mlp_body, final, 1024),
        grid=grid, in_specs=[row, vec, full(wu), full(wd), vec], out_specs=row,
        out_shape=jax.ShapeDtypeStruct((t_rows, D_MODEL), F32),
        compiler_params=pltpu.CompilerParams(
            dimension_semantics=("arbitrary",), vmem_limit_bytes=VMEM_LIMIT),
        name="mlp",
    )(x2, g2, wu, wd, gf)


def _rope_tables(seq):
    pos = jnp.arange(seq, dtype=F32)
    inv_freq = ROPE_THETA ** (-jnp.arange(0, ROPE_DIM, 2, dtype=F32) / ROPE_DIM)
    ang = pos[:, None] * inv_freq[None, :]
    cos, sin = jnp.cos(ang), jnp.sin(ang)
    ones = jnp.ones((seq, ATT_DIM - ROPE_DIM), F32)
    zeros_h = jnp.zeros((seq, ROPE_HALF), F32)
    zeros_r = jnp.zeros((seq, ATT_DIM - ROPE_DIM), F32)
    c = jnp.concatenate([cos, cos, ones], axis=1)
    a = jnp.concatenate([zeros_h, sin, zeros_r], axis=1)
    b = jnp.concatenate([-sin, zeros_h, zeros_r], axis=1)
    scale = ATT_DIM ** -0.5
    stack = lambda t: jnp.stack([t * scale, t], axis=0)
    return stack(c), stack(a), stack(b)


def kernel(x, norm1_g, w_in, hg_lower_bounds, hg_norm_g, w_branch_a, w_branch_b, w_out, norm2_g,
           w_up, w_down, final_norm_g):
    batch, seq, d = x.shape
    depth = w_in.shape[0]
    t_rows = batch * seq
    tm = 512
    tm_in = 256
    rope_c, rope_a, rope_b = _rope_tables(seq)
    sel, masks = _hgrn_tables()
    head_id = np.arange(ATT_DIM)[:, None]
    col_head = np.arange(ATT_W)[None, :] // ATT_DIM
    expand1 = (head_id == col_head).astype(np.float32)
    expand = jnp.asarray(np.concatenate([expand1, expand1], axis=0), BF16)
    lb_raw = hg_lower_bounds.astype(F32)
    gf = final_norm_g.reshape(1, d).astype(F32)

    x2 = x.reshape(t_rows, d)
    for l in range(depth):
        q, lf, k, vg, a0, a1, a2, gab = _inproj(
            x2, norm1_g[l].reshape(1, d), w_in[l].astype(BF16), lb_raw, rope_c, rope_a, rope_b,
            l, batch, seq, tm_in)
        ohg = _hgrn(q, lf, k, vg, sel, masks, hg_norm_g[l].reshape(1, HG_W), batch, seq, 512)
        outs, lses = [], []
        for grp, (qkv_g, n_blk) in enumerate(zip((a0, a1, a2), ATT_BLOCKS_PER_STEP)):
            o_g, lse_g = _attn_group(qkv_g, grp, batch, seq, n_blk)
            outs.append(o_g)
            lses.append(lse_g)
        x2 = _merge(x2, ohg, outs, lses, gab, expand, w_branch_a[l].astype(BF16),
                    w_branch_b[l].astype(BF16), w_out[l].astype(BF16), tm)
        x2 = _mlp(x2, norm2_g[l].reshape(1, d), w_up[l].astype(BF16), w_down[l].astype(BF16), gf,
                  l == depth - 1, tm)
    return x2.reshape(batch, seq, d)
```

```python
import functools

import numpy as np
import jax
import jax.numpy as jnp
from jax import lax
from jax.experimental import pallas as pl
from jax.experimental.pallas import tpu as pltpu

F32 = jnp.float32
BF16 = jnp.bfloat16

D_MODEL = 1024
HG_HEADS = 4
HG_DIM = 128
HG_W = HG_HEADS * HG_DIM
ATT_GROUPS = ((128, 1), (512, 4), (2048, 16))
N_GROUPS = len(ATT_GROUPS)
ATT_HEADS = 4
ATT_DIM = 128
ATT_W = ATT_HEADS * ATT_DIM
ATT_QKV_W = 3 * N_GROUPS * ATT_W
ATT_L = 128
ROPE_THETA = 500000.0
ROPE_DIM = ATT_DIM // 4
ROPE_HALF = ROPE_DIM // 2
D_FF = 4 * D_MODEL
EPS = 1e-6
N_IN = 4 * HG_W + ATT_QKV_W + 2 * D_MODEL

COL_TILE = 512
HG_CHUNK = 64
HG_LEVELS = 6
NEG_BIG = -1e30
ATT_BLOCKS_PER_STEP = (8, 2, 1)
ATT_RESIDUE_UNROLL = (1, 4, 4)
VMEM_LIMIT = 56 * 1024 * 1024


def _sigmoid(z):
    return 0.5 * jnp.tanh(0.5 * z) + 0.5


def _dot(a, b):
    return jnp.dot(a, b, preferred_element_type=F32)


def _dot_nt(a, b):
    return lax.dot_general(a, b, (((1,), (1,)), ((), ())), preferred_element_type=F32)


def _dot_tn(a, b):
    return lax.dot_general(a, b, (((0,), (0,)), ((), ())), preferred_element_type=F32)


def _inproj_body(layer, tm, x_ref, g1_ref, w_ref, lb_ref, rc_ref, rs_ref,
                 q_ref, lf_ref, k_ref, vg_ref, a0_ref, a1_ref, a2_ref, gab_ref, scr_ref):
    x = x_ref[...]
    ms = jnp.mean(x * x, axis=-1, keepdims=True)
    h = (x * lax.rsqrt(ms + EPS) * g1_ref[...]).astype(BF16)

    def proj(tile):
        return _dot(h, w_ref[:, tile * COL_TILE:(tile + 1) * COL_TILE])

    acc = proj(0)
    q_ref[...] = acc * _sigmoid(acc)

    acc = proj(1)
    raw = lb_ref[...]
    rows = [raw[r:r + 1, :] for r in range(raw.shape[0])]
    mx = functools.reduce(jnp.maximum, rows)
    ex = [jnp.exp(r - mx) for r in rows]
    den = functools.reduce(lambda a, b: a + b, ex)
    sm = [e / den for e in ex]
    lb = functools.reduce(lambda a, b: a + b, sm[:layer + 1]) - sm[0]
    log_sig = jnp.minimum(acc, 0.0) - jnp.log1p(jnp.exp(-jnp.abs(acc)))
    a = jnp.log(lb)
    c = jnp.log1p(-lb) + log_sig
    lf_ref[...] = jnp.maximum(a, c) + jnp.log1p(jnp.exp(-jnp.abs(a - c)))
    k_ref[...] = (1.0 - lb) / (1.0 + jnp.exp(acc))

    vg_ref[:, :HG_W] = proj(2).astype(BF16)
    vg_ref[:, HG_W:] = _sigmoid(proj(3)).astype(BF16)

    att_refs = (a0_ref, a1_ref, a2_ref)
    n_split = 0
    for grp in range(N_GROUPS):
        dil = ATT_GROUPS[grp][1]
        for which in range(3):
            acc = proj(4 + which * N_GROUPS + grp)
            pieces = [acc[:, hd * ATT_DIM:(hd + 1) * ATT_DIM] for hd in range(ATT_HEADS)]
            if which < 2:
                pieces = [t * rc_ref[which] + pltpu.roll(t, ATT_DIM // 2, 1) * rs_ref[which]
                          for t in pieces]
            if dil == 1:
                for hd in range(ATT_HEADS):
                    c0 = which * ATT_W + hd * ATT_DIM
                    att_refs[grp][0, :, c0:c0 + ATT_DIM] = pieces[hd].astype(BF16)
            else:
                slot = n_split % scr_ref.shape[0]
                n_split += 1
                for hd in range(ATT_HEADS):
                    scr_ref[slot, hd] = pieces[hd]
                for r in range(dil):
                    for hd in range(ATT_HEADS):
                        c0 = which * ATT_W + hd * ATT_DIM
                        att_refs[grp][r, :, c0:c0 + ATT_DIM] = scr_ref[
                            slot, hd, pl.ds(r, tm // dil, stride=dil), :].astype(BF16)

    for t in range(4):
        gab_ref[:, t * COL_TILE:(t + 1) * COL_TILE] = _sigmoid(proj(13 + t)).astype(BF16)


def _inproj(x2, g1, w_bf, lb_raw, rope_c, rope_s, layer, batch, seq, tm):
    t_rows = x2.shape[0]
    s_tiles = seq // tm
    grid = (t_rows // tm,)
    rope_spec = pl.BlockSpec((2, tm, ATT_DIM), lambda i: (0, i % s_tiles, 0))
    in_specs = [
        pl.BlockSpec((tm, D_MODEL), lambda i: (i, 0)),
        pl.BlockSpec((1, D_MODEL), lambda i: (0, 0)),
        pl.BlockSpec((D_MODEL, N_IN), lambda i: (0, 0), pipeline_mode=pl.Buffered(1)),
        pl.BlockSpec(lb_raw.shape, lambda i: (0, 0)),
        rope_spec, rope_spec,
    ]
    row = lambda w: pl.BlockSpec((tm, w), lambda i: (i, 0))
    att_spec = lambda dil: pl.BlockSpec(
        (None, dil, tm // dil, 3 * ATT_W), lambda i: (i // s_tiles, 0, i % s_tiles, 0))
    out_specs = [row(HG_W), row(HG_W), row(HG_W), row(2 * HG_W)]
    out_specs += [att_spec(dil) for _, dil in ATT_GROUPS]
    out_specs += [row(2 * D_MODEL)]
    out_shape = [
        jax.ShapeDtypeStruct((t_rows, HG_W), F32),
        jax.ShapeDtypeStruct((t_rows, HG_W), F32),
        jax.ShapeDtypeStruct((t_rows, HG_W), F32),
        jax.ShapeDtypeStruct((t_rows, 2 * HG_W), BF16),
    ]
    out_shape += [jax.ShapeDtypeStruct((batch, dil, seq // dil, 3 * ATT_W), BF16)
                  for _, dil in ATT_GROUPS]
    out_shape += [jax.ShapeDtypeStruct((t_rows, 2 * D_MODEL), BF16)]
    return pl.pallas_call(
        functools.partial(_inproj_body, layer, tm),
        grid=grid, in_specs=in_specs, out_specs=out_specs, out_shape=out_shape,
        scratch_shapes=[pltpu.VMEM((2, ATT_HEADS, tm, ATT_DIM), F32)],
        compiler_params=pltpu.CompilerParams(
            dimension_semantics=("arbitrary",), vmem_limit_bytes=VMEM_LIMIT),
        name="inproj",
    )(x2, g1, w_bf, lb_raw, rope_c, rope_s)


def _hgrn_tables():
    c = HG_CHUNK
    slabs = []
    t = np.arange(c)[:, None]
    u = np.arange(c)[None, :]
    slabs.append((u <= t))
    masks = [np.eye(c, dtype=bool)]
    for lvl in range(HG_LEVELS):
        beta = 1 << lvl
        mid = (t // (2 * beta)) * (2 * beta) + beta - 1
        right = t > mid
        sel = np.where(right, (u > mid) & (u <= t), (u > t) & (u <= mid))
        slabs.append(sel)
        s = np.arange(c)[None, :]
        same = (t // (2 * beta)) == (s // (2 * beta))
        s_left = (s % (2 * beta)) < beta
        masks.append(same & right & s_left)
    slabs.append(u > t)
    m = np.concatenate(slabs, axis=0).astype(np.float32)
    m3 = np.concatenate([m, m, m], axis=1)
    return jnp.asarray(m3, BF16), jnp.asarray(np.stack(masks).astype(np.float32))


def _hgrn_body(n_chunks, q_ref, lf_ref, k_ref, vg_ref, sel_ref, mask_ref, gn_ref, o_ref, st_ref):
    c = HG_CHUNK

    @pl.when(pl.program_id(1) == 0)
    def _():
        st_ref[...] = jnp.zeros_like(st_ref)

    row_right = []
    rid = lax.broadcasted_iota(jnp.int32, (c, 1), 0)
    for lvl in range(HG_LEVELS):
        row_right.append(((rid >> lvl) & 1) == 1)

    def chunk(ci, carry):
        r0 = pl.multiple_of(ci * c, c)
        rows = pl.ds(r0, c)
        g = lf_ref[rows, :]
        g_hi = g.astype(BF16)
        r1 = g - g_hi.astype(F32)
        g_mid = r1.astype(BF16)
        g_lo = (r1 - g_mid.astype(F32)).astype(BF16)
        g3 = jnp.concatenate([g_hi, g_mid, g_lo], axis=0)
        e_all = jnp.exp(_dot(sel_ref[...], g3))
        q = q_ref[rows, :]
        k = k_ref[rows, :]
        for h in range(HG_HEADS):
            hs = slice(h * HG_DIM, (h + 1) * HG_DIM)
            qh = q[:, hs]
            kh = k[:, hs]
            vh = vg_ref[rows, hs]
            a = mask_ref[0] * _dot_nt(qh.astype(BF16), kh.astype(BF16))
            for lvl in range(HG_LEVELS):
                e = e_all[(lvl + 1) * c:(lvl + 2) * c, hs]
                ql = jnp.where(row_right[lvl], qh * e, 0.0).astype(BF16)
                kl = jnp.where(row_right[lvl], 0.0, kh * e).astype(BF16)
                a = a + mask_ref[lvl + 1] * _dot_nt(ql, kl)
            e_b = e_all[0:c, hs]
            e_last = e_all[(HG_LEVELS + 1) * c:(HG_LEVELS + 2) * c, hs]
            st = st_ref[h]
            o = _dot(a.astype(BF16), vh) + _dot_nt((qh * e_b).astype(BF16), st.astype(BF16))
            st_ref[h] = st * e_b[c - 1:c, :] + _dot_tn(vh, (kh * e_last).astype(BF16))
            ms = jnp.mean(o * o, axis=-1, keepdims=True)
            gate = vg_ref[rows, HG_W + h * HG_DIM:HG_W + (h + 1) * HG_DIM].astype(F32)
            o_ref[rows, hs] = (o * lax.rsqrt(ms + EPS) * gn_ref[:, hs] * gate).astype(BF16)
        return carry

    lax.fori_loop(0, n_chunks, chunk, 0)


def _hgrn(q, lf, k, vg, sel, masks, gn, batch, seq, rows):
    t_rows = q.shape[0]
    n_blk = seq // rows
    grid = (batch, n_blk)
    row_map = lambda b, n: (b * n_blk + n, 0)
    in_specs = [
        pl.BlockSpec((rows, HG_W), row_map),
        pl.BlockSpec((rows, HG_W), row_map),
        pl.BlockSpec((rows, HG_W), row_map),
        pl.BlockSpec((rows, 2 * HG_W), row_map),
        pl.BlockSpec(sel.shape, lambda b, n: (0, 0)),
        pl.BlockSpec(masks.shape, lambda b, n: (0, 0, 0)),
        pl.BlockSpec((1, HG_W), lambda b, n: (0, 0)),
    ]
    return pl.pallas_call(
        functools.partial(_hgrn_body, rows // HG_CHUNK),
        grid=grid, in_specs=in_specs,
        out_specs=pl.BlockSpec((rows, HG_W), row_map),
        out_shape=jax.ShapeDtypeStruct((t_rows, HG_W), BF16),
        scratch_shapes=[pltpu.VMEM((HG_HEADS, HG_DIM, HG_DIM), F32)],
        compiler_params=pltpu.CompilerParams(
            dimension_semantics=("arbitrary", "arbitrary"), vmem_limit_bytes=VMEM_LIMIT),
        name="hgrn2",
    )(q, lf, k, vg, sel, masks, gn)


def _attn_body(dil, n_blk, res_unroll, q_ref, kc_ref, kp_ref, vc_ref, vp_ref, o_ref, lse_ref, o_scr):
    n = pl.program_id(1)
    blk = ATT_L
    i = lax.broadcasted_iota(jnp.int32, (blk, 2 * blk), 0)
    j = lax.broadcasted_iota(jnp.int32, (blk, 2 * blk), 1)
    band = (j >= i) & (j <= i + blk)
    band_first = band & ((n > 0) | (j >= blk))
    lane = lax.broadcasted_iota(jnp.int32, (blk, ATT_DIM), 1)

    def residue(r, carry):
        for jb in range(n_blk):
            cur = slice(jb * blk, (jb + 1) * blk)
            prev = slice((jb - 1) * blk, jb * blk)
            valid = band_first if jb == 0 else band
            lse_tile = jnp.zeros((blk, ATT_DIM), F32)
            out_rows = pl.ds(jb * blk * dil + r, blk, stride=dil)
            for h in range(ATT_HEADS):
                hs = slice(h * ATT_DIM, (h + 1) * ATT_DIM)
                q = q_ref[r, cur, hs]
                if jb == 0:
                    k_prev, v_prev = kp_ref[r, :, hs], vp_ref[r, :, hs]
                else:
                    k_prev, v_prev = kc_ref[r, prev, hs], vc_ref[r, prev, hs]
                kk = jnp.concatenate([k_prev, kc_ref[r, cur, hs]], axis=0)
                vv = jnp.concatenate([v_prev, vc_ref[r, cur, hs]], axis=0)
                s = jnp.where(valid, _dot_nt(q, kk), NEG_BIG)
                m = jnp.max(s, axis=-1, keepdims=True)
                p = jnp.exp(s - m)
                den = jnp.sum(p, axis=-1, keepdims=True)
                o_scr[h, out_rows, :] = _dot(p.astype(BF16), vv) / den
                lse_tile = jnp.where(lane == h, m + jnp.log(den), lse_tile)
            lse_ref[out_rows, :] = lse_tile
        return carry

    def residues(it, carry):
        for rr in range(res_unroll):
            residue(it * res_unroll + rr, carry)
        return carry

    if dil == res_unroll:
        residues(0, 0)
    else:
        lax.fori_loop(0, dil // res_unroll, residues, 0)
    for h in range(ATT_HEADS):
        o_ref[:, h * ATT_DIM:(h + 1) * ATT_DIM] = o_scr[h].astype(o_ref.dtype)


def _attn_group(qkv_g, grp, batch, seq, n_blk, res_unroll):
    _, dil = ATT_GROUPS[grp]
    m_rows = seq // dil
    steps = m_rows // (ATT_L * n_blk)
    span = ATT_L * n_blk * dil
    cur = lambda c: pl.BlockSpec((None, dil, ATT_L * n_blk, ATT_W), lambda b, n: (b, 0, n, c))
    prev = lambda c: pl.BlockSpec(
        (None, dil, ATT_L, ATT_W), lambda b, n: (b, 0, jnp.maximum(n * n_blk - 1, 0), c))
    in_specs = [cur(0), cur(1), prev(1), cur(2), prev(2)]
    out_specs = [
        pl.BlockSpec((None, span, ATT_W), lambda b, n: (b, n, 0)),
        pl.BlockSpec((None, span, ATT_DIM), lambda b, n: (b, n, 0)),
    ]
    out_shape = [
        jax.ShapeDtypeStruct((batch, seq, ATT_W), BF16),
        jax.ShapeDtypeStruct((batch, seq, ATT_DIM), F32),
    ]
    o, lse = pl.pallas_call(
        functools.partial(_attn_body, dil, n_blk, res_unroll),
        grid=(batch, steps), in_specs=in_specs, out_specs=out_specs, out_shape=out_shape,
        scratch_shapes=[pltpu.VMEM((ATT_HEADS, span, ATT_DIM), F32)],
        compiler_params=pltpu.CompilerParams(
            dimension_semantics=("arbitrary", "arbitrary"), vmem_limit_bytes=VMEM_LIMIT),
        name=f"attn_g{grp}",
    )(qkv_g, qkv_g, qkv_g, qkv_g, qkv_g)
    return o.reshape(batch * seq, ATT_W), lse.reshape(batch * seq, ATT_DIM)


def _merge_body(x_ref, ohg_ref, o0_ref, o1_ref, o2_ref, l0_ref, l1_ref, l2_ref, gab_ref,
                exp_ref, wa_ref, wb_ref, wo_ref, out_ref):
    lses = [l0_ref[...], l1_ref[...], l2_ref[...]]
    mx = jnp.maximum(jnp.maximum(lses[0], lses[1]), lses[2])
    es = [jnp.exp(l - mx) for l in lses]
    inv = 1.0 / (es[0] + es[1] + es[2])
    o_att = None
    for e, o_ref in zip(es, (o0_ref, o1_ref, o2_ref)):
        w = e * inv
        w_hi = w.astype(BF16)
        w_lo = (w - w_hi.astype(F32)).astype(BF16)
        w_full = _dot(jnp.concatenate([w_hi, w_lo], axis=1), exp_ref[...])
        term = w_full * o_ref[...].astype(F32)
        o_att = term if o_att is None else o_att + term
    ga = gab_ref[:, :D_MODEL].astype(F32)
    gb = gab_ref[:, D_MODEL:].astype(F32)
    y = ga * _dot(ohg_ref[...], wa_ref[...]) + gb * _dot(o_att.astype(BF16), wb_ref[...])
    out_ref[...] = x_ref[...] + _dot(y.astype(BF16), wo_ref[...])


def _merge(x2, ohg, outs, lses, gab, expand, wa, wb, wo, tm):
    t_rows = x2.shape[0]
    grid = (t_rows // tm,)
    row = lambda w: pl.BlockSpec((tm, w), lambda i: (i, 0))
    full = lambda a: pl.BlockSpec(a.shape, lambda i: (0, 0))
    in_specs = [row(D_MODEL), row(HG_W), row(ATT_W), row(ATT_W), row(ATT_W),
                row(ATT_DIM), row(ATT_DIM), row(ATT_DIM), row(2 * D_MODEL),
                full(expand), full(wa), full(wb), full(wo)]
    return pl.pallas_call(
        _merge_body, grid=grid, in_specs=in_specs, out_specs=row(D_MODEL),
        out_shape=jax.ShapeDtypeStruct((t_rows, D_MODEL), F32),
        compiler_params=pltpu.CompilerParams(
            dimension_semantics=("arbitrary",), vmem_limit_bytes=VMEM_LIMIT),
        name="merge",
    )(x2, ohg, outs[0], outs[1], outs[2], lses[0], lses[1], lses[2], gab, expand, wa, wb, wo)


def _mlp_body(final, ff_tile, x_ref, g2_ref, wu_ref, wd_ref, gf_ref, out_ref):
    x = x_ref[...]
    ms = jnp.mean(x * x, axis=-1, keepdims=True)
    h = (x * lax.rsqrt(ms + EPS) * g2_ref[...]).astype(BF16)
    acc = x
    for c in range(D_FF // ff_tile):
        cs = slice(c * ff_tile, (c + 1) * ff_tile)
        u = jnp.maximum(_dot(h, wu_ref[:, cs]), 0.0)
        acc = acc + _dot((u * u).astype(BF16), wd_ref[cs, :])
    if final:
        ms2 = jnp.mean(acc * acc, axis=-1, keepdims=True)
        acc = acc * lax.rsqrt(ms2 + EPS) * gf_ref[...]
    out_ref[...] = acc


def _mlp(x2, g2, wu, wd, gf, final, tm):
    t_rows = x2.shape[0]
    grid = (t_rows // tm,)
    row = pl.BlockSpec((tm, D_MODEL), lambda i: (i, 0))
    vec = pl.BlockSpec((1, D_MODEL), lambda i: (0, 0))
    full = lambda a: pl.BlockSpec(a.shape, lambda i: (0, 0))
    return pl.pallas_call(
        functools.partial(_mlp_body, final, 1024),
        grid=grid, in_specs=[row, vec, full(wu), full(wd), vec], out_specs=row,
        out_shape=jax.ShapeDtypeStruct((t_rows, D_MODEL), F32),
        compiler_params=pltpu.CompilerParams(
            dimension_semantics=("arbitrary",), vmem_limit_bytes=VMEM_LIMIT),
        name="mlp",
    )(x2, g2, wu, wd, gf)


def _rope_perm():
    d = np.arange(ATT_DIM)
    half = ATT_DIM // 2
    plain = d[ROPE_DIM:]
    n_first = half - ROPE_HALF
    return np.concatenate([d[:ROPE_HALF], plain[:n_first], d[ROPE_HALF:ROPE_DIM], plain[n_first:]])


def _inproj_col_perm():
    cols = np.arange(N_IN)
    head_perm = _rope_perm()
    base = 4 * HG_W
    for which in range(2):
        for grp in range(N_GROUPS):
            for hd in range(ATT_HEADS):
                c0 = base + (which * N_GROUPS + grp) * ATT_W + hd * ATT_DIM
                cols[c0:c0 + ATT_DIM] = c0 + head_perm
    return cols


def _rope_tables(seq):
    pos = jnp.arange(seq, dtype=F32)
    inv_freq = ROPE_THETA ** (-jnp.arange(0, ROPE_DIM, 2, dtype=F32) / ROPE_DIM)
    ang = pos[:, None] * inv_freq[None, :]
    cos, sin = jnp.cos(ang), jnp.sin(ang)
    n_plain = ATT_DIM // 2 - ROPE_HALF
    ones = jnp.ones((seq, n_plain), F32)
    zeros = jnp.zeros((seq, n_plain), F32)
    c = jnp.concatenate([cos, ones, cos, ones], axis=1)
    s = jnp.concatenate([-sin, zeros, sin, zeros], axis=1)
    scale = ATT_DIM ** -0.5
    stack = lambda t: jnp.stack([t * scale, t], axis=0)
    return stack(c), stack(s)


def kernel(x, norm1_g, w_in, hg_lower_bounds, hg_norm_g, w_branch_a, w_branch_b, w_out, norm2_g,
           w_up, w_down, final_norm_g):
    batch, seq, d = x.shape
    depth = w_in.shape[0]
    t_rows = batch * seq
    tm = 512
    tm_in = 512
    rope_c, rope_s = _rope_tables(seq)
    col_perm = _inproj_col_perm()
    sel, masks = _hgrn_tables()
    head_id = np.arange(ATT_DIM)[:, None]
    col_head = np.arange(ATT_W)[None, :] // ATT_DIM
    expand1 = (head_id == col_head).astype(np.float32)
    expand = jnp.asarray(np.concatenate([expand1, expand1], axis=0), BF16)
    lb_raw = hg_lower_bounds.astype(F32)
    gf = final_norm_g.reshape(1, d).astype(F32)

    x2 = x.reshape(t_rows, d)
    for l in range(depth):
        q, lf, k, vg, a0, a1, a2, gab = _inproj(
            x2, norm1_g[l].reshape(1, d), w_in[l][:, col_perm].astype(BF16), lb_raw, rope_c, rope_s,
            l, batch, seq, tm_in)
        ohg = _hgrn(q, lf, k, vg, sel, masks, hg_norm_g[l].reshape(1, HG_W), batch, seq, 512)
        outs, lses = [], []
        for grp, qkv_g in enumerate((a0, a1, a2)):
            o_g, lse_g = _attn_group(qkv_g, grp, batch, seq, ATT_BLOCKS_PER_STEP[grp],
                                     ATT_RESIDUE_UNROLL[grp])
            outs.append(o_g)
            lses.append(lse_g)
        x2 = _merge(x2, ohg, outs, lses, gab, expand, w_branch_a[l].astype(BF16),
                    w_branch_b[l].astype(BF16), w_out[l].astype(BF16), tm)
        x2 = _mlp(x2, norm2_g[l].reshape(1, d), w_up[l].astype(BF16), w_down[l].astype(BF16), gf,
                  l == depth - 1, tm)
    return x2.reshape(batch, seq, d)
```

```python
import functools

import numpy as np
import jax
import jax.numpy as jnp
from jax import lax
from jax.experimental import pallas as pl
from jax.experimental.pallas import tpu as pltpu

F32 = jnp.float32
BF16 = jnp.bfloat16

D_MODEL = 1024
HG_HEADS = 4
HG_DIM = 128
HG_W = HG_HEADS * HG_DIM
ATT_GROUPS = ((128, 1), (512, 4), (2048, 16))
N_GROUPS = len(ATT_GROUPS)
ATT_HEADS = 4
ATT_DIM = 128
ATT_W = ATT_HEADS * ATT_DIM
ATT_QKV_W = 3 * N_GROUPS * ATT_W
ATT_L = 128
ROPE_THETA = 500000.0
ROPE_DIM = ATT_DIM // 4
ROPE_HALF = ROPE_DIM // 2
D_FF = 4 * D_MODEL
EPS = 1e-6
N_IN = 4 * HG_W + ATT_QKV_W + 2 * D_MODEL

COL_TILE = 512
HG_CHUNK = 128
HG_LEVELS = 7
HG_SEL_LEVELS = 3
NEG_BIG = -1e30
ATT_BLOCKS_PER_STEP = (8, 2, 1)
ATT_RESIDUE_UNROLL = (1, 4, 4)
VMEM_LIMIT = 56 * 1024 * 1024


def _sigmoid(z):
    return 0.5 * jnp.tanh(0.5 * z) + 0.5


def _dot(a, b):
    return jnp.dot(a, b, preferred_element_type=F32)


def _dot_nt(a, b):
    return lax.dot_general(a, b, (((1,), (1,)), ((), ())), preferred_element_type=F32)


def _dot_tn(a, b):
    return lax.dot_general(a, b, (((0,), (0,)), ((), ())), preferred_element_type=F32)


def _inproj_body(layer, tm, x_ref, g1_ref, w_ref, lb_ref, rc_ref, rs_ref,
                 q_ref, lf_ref, k_ref, vg_ref, a0_ref, a1_ref, a2_ref, gab_ref, scr_ref):
    x = x_ref[...]
    ms = jnp.mean(x * x, axis=-1, keepdims=True)
    h = (x * lax.rsqrt(ms + EPS) * g1_ref[...]).astype(BF16)

    def proj(tile):
        return _dot(h, w_ref[:, tile * COL_TILE:(tile + 1) * COL_TILE])

    acc = proj(0)
    q_ref[...] = acc * _sigmoid(acc)

    acc = proj(1)
    raw = lb_ref[...]
    rows = [raw[r:r + 1, :] for r in range(raw.shape[0])]
    mx = functools.reduce(jnp.maximum, rows)
    ex = [jnp.exp(r - mx) for r in rows]
    den = functools.reduce(lambda a, b: a + b, ex)
    sm = [e / den for e in ex]
    lb = functools.reduce(lambda a, b: a + b, sm[:layer + 1]) - sm[0]
    log_sig = jnp.minimum(acc, 0.0) - jnp.log1p(jnp.exp(-jnp.abs(acc)))
    a = jnp.log(lb)
    c = jnp.log1p(-lb) + log_sig
    lf_ref[...] = jnp.maximum(a, c) + jnp.log1p(jnp.exp(-jnp.abs(a - c)))
    k_ref[...] = (1.0 - lb) / (1.0 + jnp.exp(acc))

    vg_ref[:, :HG_W] = proj(2).astype(BF16)
    vg_ref[:, HG_W:] = _sigmoid(proj(3)).astype(BF16)

    att_refs = (a0_ref, a1_ref, a2_ref)
    n_split = 0
    for grp in range(N_GROUPS):
        dil = ATT_GROUPS[grp][1]
        for which in range(3):
            acc = proj(4 + which * N_GROUPS + grp)
            pieces = [acc[:, hd * ATT_DIM:(hd + 1) * ATT_DIM] for hd in range(ATT_HEADS)]
            if which < 2:
                pieces = [t * rc_ref[which] + pltpu.roll(t, ATT_DIM // 2, 1) * rs_ref[which]
                          for t in pieces]
            if dil == 1:
                for hd in range(ATT_HEADS):
                    c0 = which * ATT_W + hd * ATT_DIM
                    att_refs[grp][0, :, c0:c0 + ATT_DIM] = pieces[hd].astype(BF16)
            else:
                slot = n_split % scr_ref.shape[0]
                n_split += 1
                for hd in range(ATT_HEADS):
                    scr_ref[slot, hd] = pieces[hd]
                for r in range(dil):
                    for hd in range(ATT_HEADS):
                        c0 = which * ATT_W + hd * ATT_DIM
                        att_refs[grp][r, :, c0:c0 + ATT_DIM] = scr_ref[
                            slot, hd, pl.ds(r, tm // dil, stride=dil), :].astype(BF16)

    for t in range(4):
        gab_ref[:, t * COL_TILE:(t + 1) * COL_TILE] = _sigmoid(proj(13 + t)).astype(BF16)


def _inproj(x2, g1, w_bf, lb_raw, rope_c, rope_s, layer, batch, seq, tm):
    t_rows = x2.shape[0]
    s_tiles = seq // tm
    grid = (t_rows // tm,)
    rope_spec = pl.BlockSpec((2, tm, ATT_DIM), lambda i: (0, i % s_tiles, 0))
    in_specs = [
        pl.BlockSpec((tm, D_MODEL), lambda i: (i, 0)),
        pl.BlockSpec((1, D_MODEL), lambda i: (0, 0)),
        pl.BlockSpec((D_MODEL, N_IN), lambda i: (0, 0), pipeline_mode=pl.Buffered(1)),
        pl.BlockSpec(lb_raw.shape, lambda i: (0, 0)),
        rope_spec, rope_spec,
    ]
    row = lambda w: pl.BlockSpec((tm, w), lambda i: (i, 0))
    att_spec = lambda dil: pl.BlockSpec(
        (None, dil, tm // dil, 3 * ATT_W), lambda i: (i // s_tiles, 0, i % s_tiles, 0))
    out_specs = [row(HG_W), row(HG_W), row(HG_W), row(2 * HG_W)]
    out_specs += [att_spec(dil) for _, dil in ATT_GROUPS]
    out_specs += [row(2 * D_MODEL)]
    out_shape = [
        jax.ShapeDtypeStruct((t_rows, HG_W), F32),
        jax.ShapeDtypeStruct((t_rows, HG_W), F32),
        jax.ShapeDtypeStruct((t_rows, HG_W), F32),
        jax.ShapeDtypeStruct((t_rows, 2 * HG_W), BF16),
    ]
    out_shape += [jax.ShapeDtypeStruct((batch, dil, seq // dil, 3 * ATT_W), BF16)
                  for _, dil in ATT_GROUPS]
    out_shape += [jax.ShapeDtypeStruct((t_rows, 2 * D_MODEL), BF16)]
    return pl.pallas_call(
        functools.partial(_inproj_body, layer, tm),
        grid=grid, in_specs=in_specs, out_specs=out_specs, out_shape=out_shape,
        scratch_shapes=[pltpu.VMEM((2, ATT_HEADS, tm, ATT_DIM), F32)],
        compiler_params=pltpu.CompilerParams(
            dimension_semantics=("arbitrary",), vmem_limit_bytes=VMEM_LIMIT),
        name="inproj",
    )(x2, g1, w_bf, lb_raw, rope_c, rope_s)


def _hgrn_tables():
    c = HG_CHUNK
    t = np.arange(c)[:, None]
    u = np.arange(c)[None, :]
    slabs = [u <= t]
    masks = [np.eye(c, dtype=bool)]
    for lvl in range(HG_LEVELS):
        beta = 1 << lvl
        mid = (t // (2 * beta)) * (2 * beta) + beta - 1
        right = t > mid
        if lvl < HG_SEL_LEVELS:
            slabs.append(np.where(right, (u > mid) & (u <= t), (u > t) & (u <= mid)))
        same = (t // (2 * beta)) == (u // (2 * beta))
        masks.append(same & right & ((u % (2 * beta)) < beta))
    m = np.concatenate(slabs, axis=0).astype(np.float32)
    m3 = np.concatenate([m, m, m], axis=1)
    return jnp.asarray(m3, BF16), jnp.asarray(np.stack(masks).astype(np.float32))


def _hgrn_body(n_chunks, q_ref, lf_ref, k_ref, vg_ref, sel_ref, mask_ref, gn_ref, o_ref, st_ref):
    c = HG_CHUNK

    @pl.when(pl.program_id(1) == 0)
    def _():
        st_ref[...] = jnp.zeros_like(st_ref)

    rid = lax.broadcasted_iota(jnp.int32, (c, 1), 0)
    row_right = [((rid >> lvl) & 1) == 1 for lvl in range(HG_LEVELS)]

    for ci in range(n_chunks):
        rows = slice(ci * c, (ci + 1) * c)
        g = lf_ref[rows, :]
        g_hi = g.astype(BF16)
        r1 = g - g_hi.astype(F32)
        g_mid = r1.astype(BF16)
        g_lo = (r1 - g_mid.astype(F32)).astype(BF16)
        g3 = jnp.concatenate([g_hi, g_mid, g_lo], axis=0)
        sums = _dot(sel_ref[...], g3)
        b = sums[0:c]
        e_lvl = [jnp.exp(sums[(lvl + 1) * c:(lvl + 2) * c]) for lvl in range(HG_SEL_LEVELS)]
        for lvl in range(HG_SEL_LEVELS, HG_LEVELS):
            beta = 1 << lvl
            parts = []
            for g0 in range(0, c, 2 * beta):
                b_mid = b[g0 + beta - 1:g0 + beta, :]
                parts.append(b_mid - b[g0:g0 + beta])
                parts.append(b[g0 + beta:g0 + 2 * beta] - b_mid)
            e_lvl.append(jnp.exp(jnp.concatenate(parts, axis=0)))
        e_b = jnp.exp(b)
        e_last = jnp.exp(b[c - 1:c, :] - b)
        q = q_ref[rows, :]
        k = k_ref[rows, :]
        for h in range(HG_HEADS):
            hs = slice(h * HG_DIM, (h + 1) * HG_DIM)
            qh = q[:, hs]
            kh = k[:, hs]
            vh = vg_ref[rows, hs]
            a = mask_ref[0] * _dot_nt(qh.astype(BF16), kh.astype(BF16))
            for lvl in range(HG_LEVELS):
                e = e_lvl[lvl][:, hs]
                ql = jnp.where(row_right[lvl], qh * e, 0.0).astype(BF16)
                kl = jnp.where(row_right[lvl], 0.0, kh * e).astype(BF16)
                a = a + mask_ref[lvl + 1] * _dot_nt(ql, kl)
            st = st_ref[h]
            o = (_dot(a.astype(BF16), vh)
                 + _dot_nt((qh * e_b[:, hs]).astype(BF16), st.astype(BF16)))
            st_ref[h] = (st * e_b[c - 1:c, hs]
                         + _dot_tn(vh, (kh * e_last[:, hs]).astype(BF16)))
            ms = jnp.mean(o * o, axis=-1, keepdims=True)
            gate = vg_ref[rows, HG_W + h * HG_DIM:HG_W + (h + 1) * HG_DIM].astype(F32)
            o_ref[rows, hs] = (o * lax.rsqrt(ms + EPS) * gn_ref[:, hs] * gate).astype(BF16)


def _hgrn(q, lf, k, vg, sel, masks, gn, batch, seq, rows):
    t_rows = q.shape[0]
    n_blk = seq // rows
    grid = (batch, n_blk)
    row_map = lambda b, n: (b * n_blk + n, 0)
    in_specs = [
        pl.BlockSpec((rows, HG_W), row_map),
        pl.BlockSpec((rows, HG_W), row_map),
        pl.BlockSpec((rows, HG_W), row_map),
        pl.BlockSpec((rows, 2 * HG_W), row_map),
        pl.BlockSpec(sel.shape, lambda b, n: (0, 0)),
        pl.BlockSpec(masks.shape, lambda b, n: (0, 0, 0)),
        pl.BlockSpec((1, HG_W), lambda b, n: (0, 0)),
    ]
    return pl.pallas_call(
        functools.partial(_hgrn_body, rows // HG_CHUNK),
        grid=grid, in_specs=in_specs,
        out_specs=pl.BlockSpec((rows, HG_W), row_map),
        out_shape=jax.ShapeDtypeStruct((t_rows, HG_W), BF16),
        scratch_shapes=[pltpu.VMEM((HG_HEADS, HG_DIM, HG_DIM), F32)],
        compiler_params=pltpu.CompilerParams(
            dimension_semantics=("arbitrary", "arbitrary"), vmem_limit_bytes=VMEM_LIMIT),
        name="hgrn2",
    )(q, lf, k, vg, sel, masks, gn)


def _attn_body(dil, n_blk, res_unroll, q_ref, kc_ref, kp_ref, vc_ref, vp_ref, o_ref, lse_ref, o_scr):
    n = pl.program_id(1)
    blk = ATT_L
    i = lax.broadcasted_iota(jnp.int32, (blk, 2 * blk), 0)
    j = lax.broadcasted_iota(jnp.int32, (blk, 2 * blk), 1)
    band = (j >= i) & (j <= i + blk)
    band_first = band & ((n > 0) | (j >= blk))
    lane = lax.broadcasted_iota(jnp.int32, (blk, ATT_DIM), 1)

    def residue(r, carry):
        for jb in range(n_blk):
            cur = slice(jb * blk, (jb + 1) * blk)
            prev = slice((jb - 1) * blk, jb * blk)
            valid = band_first if jb == 0 else band
            lse_tile = jnp.zeros((blk, ATT_DIM), F32)
            out_rows = pl.ds(jb * blk * dil + r, blk, stride=dil)
            for h in range(ATT_HEADS):
                hs = slice(h * ATT_DIM, (h + 1) * ATT_DIM)
                q = q_ref[r, cur, hs]
                if jb == 0:
                    k_prev, v_prev = kp_ref[r, :, hs], vp_ref[r, :, hs]
                else:
                    k_prev, v_prev = kc_ref[r, prev, hs], vc_ref[r, prev, hs]
                kk = jnp.concatenate([k_prev, kc_ref[r, cur, hs]], axis=0)
                vv = jnp.concatenate([v_prev, vc_ref[r, cur, hs]], axis=0)
                s = jnp.where(valid, _dot_nt(q, kk), NEG_BIG)
                m = jnp.max(s, axis=-1, keepdims=True)
                p = jnp.exp(s - m)
                den = jnp.sum(p, axis=-1, keepdims=True)
                o_scr[h, out_rows, :] = _dot(p.astype(BF16), vv) / den
                lse_tile = jnp.where(lane == h, m + jnp.log(den), lse_tile)
            lse_ref[out_rows, :] = lse_tile
        return carry

    def residues(it, carry):
        for rr in range(res_unroll):
            residue(it * res_unroll + rr, carry)
        return carry

    if dil == res_unroll:
        residues(0, 0)
    else:
        lax.fori_loop(0, dil // res_unroll, residues, 0)
    for h in range(ATT_HEADS):
        o_ref[:, h * ATT_DIM:(h + 1) * ATT_DIM] = o_scr[h].astype(o_ref.dtype)


def _attn_group(qkv_g, grp, batch, seq, n_blk, res_unroll):
    _, dil = ATT_GROUPS[grp]
    m_rows = seq // dil
    steps = m_rows // (ATT_L * n_blk)
    span = ATT_L * n_blk * dil
    cur = lambda c: pl.BlockSpec((None, dil, ATT_L * n_blk, ATT_W), lambda b, n: (b, 0, n, c))
    prev = lambda c: pl.BlockSpec(
        (None, dil, ATT_L, ATT_W), lambda b, n: (b, 0, jnp.maximum(n * n_blk - 1, 0), c))
    in_specs = [cur(0), cur(1), prev(1), cur(2), prev(2)]
    out_specs = [
        pl.BlockSpec((None, span, ATT_W), lambda b, n: (b, n, 0)),
        pl.BlockSpec((None, span, ATT_DIM), lambda b, n: (b, n, 0)),
    ]
    out_shape = [
        jax.ShapeDtypeStruct((batch, seq, ATT_W), BF16),
        jax.ShapeDtypeStruct((batch, seq, ATT_DIM), F32),
    ]
    o, lse = pl.pallas_call(
        functools.partial(_attn_body, dil, n_blk, res_unroll),
        grid=(batch, steps), in_specs=in_specs, out_specs=out_specs, out_shape=out_shape,
        scratch_shapes=[pltpu.VMEM((ATT_HEADS, span, ATT_DIM), F32)],
        compiler_params=pltpu.CompilerParams(
            dimension_semantics=("arbitrary", "arbitrary"), vmem_limit_bytes=VMEM_LIMIT),
        name=f"attn_g{grp}",
    )(qkv_g, qkv_g, qkv_g, qkv_g, qkv_g)
    return o.reshape(batch * seq, ATT_W), lse.reshape(batch * seq, ATT_DIM)


def _merge_body(x_ref, ohg_ref, o0_ref, o1_ref, o2_ref, l0_ref, l1_ref, l2_ref, gab_ref,
                exp_ref, wa_ref, wb_ref, wo_ref, out_ref):
    lses = [l0_ref[...], l1_ref[...], l2_ref[...]]
    mx = jnp.maximum(jnp.maximum(lses[0], lses[1]), lses[2])
    es = [jnp.exp(l - mx) for l in lses]
    inv = 1.0 / (es[0] + es[1] + es[2])
    o_att = None
    for e, o_ref in zip(es, (o0_ref, o1_ref, o2_ref)):
        w = e * inv
        w_hi = w.astype(BF16)
        w_lo = (w - w_hi.astype(F32)).astype(BF16)
        w_full = _dot(jnp.concatenate([w_hi, w_lo], axis=1), exp_ref[...])
        term = w_full * o_ref[...].astype(F32)
        o_att = term if o_att is None else o_att + term
    ga = gab_ref[:, :D_MODEL].astype(F32)
    gb = gab_ref[:, D_MODEL:].astype(F32)
    y = ga * _dot(ohg_ref[...], wa_ref[...]) + gb * _dot(o_att.astype(BF16), wb_ref[...])
    out_ref[...] = x_ref[...] + _dot(y.astype(BF16), wo_ref[...])


def _merge(x2, ohg, outs, lses, gab, expand, wa, wb, wo, tm):
    t_rows = x2.shape[0]
    grid = (t_rows // tm,)
    row = lambda w: pl.BlockSpec((tm, w), lambda i: (i, 0))
    full = lambda a: pl.BlockSpec(a.shape, lambda i: (0, 0))
    in_specs = [row(D_MODEL), row(HG_W), row(ATT_W), row(ATT_W), row(ATT_W),
                row(ATT_DIM), row(ATT_DIM), row(ATT_DIM), row(2 * D_MODEL),
                full(expand), full(wa), full(wb), full(wo)]
    return pl.pallas_call(
        _merge_body, grid=grid, in_specs=in_specs, out_specs=row(D_MODEL),
        out_shape=jax.ShapeDtypeStruct((t_rows, D_MODEL), F32),
        compiler_params=pltpu.CompilerParams(
            dimension_semantics=("arbitrary",), vmem_limit_bytes=VMEM_LIMIT),
        name="merge",
    )(x2, ohg, outs[0], outs[1], outs[2], lses[0], lses[1], lses[2], gab, expand, wa, wb, wo)


def _mlp_body(final, ff_tile, x_ref, g2_ref, wu_ref, wd_ref, gf_ref, out_ref):
    x = x_ref[...]
    ms = jnp.mean(x * x, axis=-1, keepdims=True)
    h = (x * lax.rsqrt(ms + EPS) * g2_ref[...]).astype(BF16)
    acc = x
    for c in range(D_FF // ff_tile):
        cs = slice(c * ff_tile, (c + 1) * ff_tile)
        u = jnp.maximum(_dot(h, wu_ref[:, cs]), 0.0)
        acc = acc + _dot((u * u).astype(BF16), wd_ref[cs, :])
    if final:
        ms2 = jnp.mean(acc * acc, axis=-1, keepdims=True)
        acc = acc * lax.rsqrt(ms2 + EPS) * gf_ref[...]
    out_ref[...] = acc


def _mlp(x2, g2, wu, wd, gf, final, tm):
    t_rows = x2.shape[0]
    grid = (t_rows // tm,)
    row = pl.BlockSpec((tm, D_MODEL), lambda i: (i, 0))
    vec = pl.BlockSpec((1, D_MODEL), lambda i: (0, 0))
    full = lambda a: pl.BlockSpec(a.shape, lambda i: (0, 0))
    return pl.pallas_call(
        functools.partial(_mlp_body, final, 1024),
        grid=grid, in_specs=[row, vec, full(wu), full(wd), vec], out_specs=row,
        out_shape=jax.ShapeDtypeStruct((t_rows, D_MODEL), F32),
        compiler_params=pltpu.CompilerParams(
            dimension_semantics=("arbitrary",), vmem_limit_bytes=VMEM_LIMIT),
        name="mlp",
    )(x2, g2, wu, wd, gf)


def _rope_perm():
    d = np.arange(ATT_DIM)
    half = ATT_DIM // 2
    plain = d[ROPE_DIM:]
    n_first = half - ROPE_HALF
    return np.concatenate([d[:ROPE_HALF], plain[:n_first], d[ROPE_HALF:ROPE_DIM], plain[n_first:]])


def _inproj_col_perm():
    cols = np.arange(N_IN)
    head_perm = _rope_perm()
    base = 4 * HG_W
    for which in range(2):
        for grp in range(N_GROUPS):
            for hd in range(ATT_HEADS):
                c0 = base + (which * N_GROUPS + grp) * ATT_W + hd * ATT_DIM
                cols[c0:c0 + ATT_DIM] = c0 + head_perm
    return cols


def _rope_tables(seq):
    pos = jnp.arange(seq, dtype=F32)
    inv_freq = ROPE_THETA ** (-jnp.arange(0, ROPE_DIM, 2, dtype=F32) / ROPE_DIM)
    ang = pos[:, None] * inv_freq[None, :]
    cos, sin = jnp.cos(ang), jnp.sin(ang)
    n_plain = ATT_DIM // 2 - ROPE_HALF
    ones = jnp.ones((seq, n_plain), F32)
    zeros = jnp.zeros((seq, n_plain), F32)
    c = jnp.concatenate([cos, ones, cos, ones], axis=1)
    s = jnp.concatenate([-sin, zeros, sin, zeros], axis=1)
    scale = ATT_DIM ** -0.5
    stack = lambda t: jnp.stack([t * scale, t], axis=0)
    return stack(c), stack(s)


def kernel(x, norm1_g, w_in, hg_lower_bounds, hg_norm_g, w_branch_a, w_branch_b, w_out, norm2_g,
           w_up, w_down, final_norm_g):
    batch, seq, d = x.shape
    depth = w_in.shape[0]
    t_rows = batch * seq
    tm = 512
    tm_in = 512
    rope_c, rope_s = _rope_tables(seq)
    col_perm = _inproj_col_perm()
    sel, masks = _hgrn_tables()
    head_id = np.arange(ATT_DIM)[:, None]
    col_head = np.arange(ATT_W)[None, :] // ATT_DIM
    expand1 = (head_id == col_head).astype(np.float32)
    expand = jnp.asarray(np.concatenate([expand1, expand1], axis=0), BF16)
    lb_raw = hg_lower_bounds.astype(F32)
    gf = final_norm_g.reshape(1, d).astype(F32)

    x2 = x.reshape(t_rows, d)
    for l in range(depth):
        q, lf, k, vg, a0, a1, a2, gab = _inproj(
            x2, norm1_g[l].reshape(1, d), w_in[l][:, col_perm].astype(BF16), lb_raw, rope_c, rope_s,
            l, batch, seq, tm_in)
        ohg = _hgrn(q, lf, k, vg, sel, masks, hg_norm_g[l].reshape(1, HG_W), batch, seq, 512)
        outs, lses = [], []
        for grp, qkv_g in enumerate((a0, a1, a2)):
            o_g, lse_g = _attn_group(qkv_g, grp, batch, seq, ATT_BLOCKS_PER_STEP[grp],
                                     ATT_RESIDUE_UNROLL[grp])
            outs.append(o_g)
            lses.append(lse_g)
        x2 = _merge(x2, ohg, outs, lses, gab, expand, w_branch_a[l].astype(BF16),
                    w_branch_b[l].astype(BF16), w_out[l].astype(BF16), tm)
        x2 = _mlp(x2, norm2_g[l].reshape(1, d), w_up[l].astype(BF16), w_down[l].astype(BF16), gf,
                  l == depth - 1, tm)
    return x2.reshape(batch, seq, d)
```

```python
import functools

import numpy as np
import jax
import jax.numpy as jnp
from jax import lax
from jax.experimental import pallas as pl
from jax.experimental.pallas import tpu as pltpu

F32 = jnp.float32
BF16 = jnp.bfloat16

D_MODEL = 1024
HG_HEADS = 4
HG_DIM = 128
HG_W = HG_HEADS * HG_DIM
ATT_GROUPS = ((128, 1), (512, 4), (2048, 16))
N_GROUPS = len(ATT_GROUPS)
ATT_HEADS = 4
ATT_DIM = 128
ATT_W = ATT_HEADS * ATT_DIM
ATT_QKV_W = 3 * N_GROUPS * ATT_W
ATT_L = 128
ROPE_THETA = 500000.0
ROPE_DIM = ATT_DIM // 4
ROPE_HALF = ROPE_DIM // 2
D_FF = 4 * D_MODEL
EPS = 1e-6
N_IN = 4 * HG_W + ATT_QKV_W + 2 * D_MODEL

COL_TILE = 512
HG_CHUNK = 128
HG_LEVELS = 7
HG_SEL_LEVELS = 3
NEG_BIG = -1e30
ATT_BLOCKS_PER_STEP = (8, 2, 1)
ATT_RESIDUE_UNROLL = (1, 4, 4)
VMEM_LIMIT = 56 * 1024 * 1024


def _sigmoid(z):
    return 0.5 * jnp.tanh(0.5 * z) + 0.5


def _dot(a, b):
    return jnp.dot(a, b, preferred_element_type=F32)


def _dot_nt(a, b):
    return lax.dot_general(a, b, (((1,), (1,)), ((), ())), preferred_element_type=F32)


def _dot_tn(a, b):
    return lax.dot_general(a, b, (((0,), (0,)), ((), ())), preferred_element_type=F32)


def _inproj_body(layer, tm, s_tiles, x_ref, g1_ref, w_ref, lb_ref, rc_ref, rs_ref,
                 sel_ref, mask_ref, gn_ref,
                 ohg_ref, a0_ref, a1_ref, a2_ref, gab_ref, scr_ref, st_ref):
    @pl.when(pl.program_id(0) % s_tiles == 0)
    def _():
        st_ref[...] = jnp.zeros_like(st_ref)

    x = x_ref[...]
    ms = jnp.mean(x * x, axis=-1, keepdims=True)
    h = (x * lax.rsqrt(ms + EPS) * g1_ref[...]).astype(BF16)

    def proj(tile):
        return _dot(h, w_ref[:, tile * COL_TILE:(tile + 1) * COL_TILE])

    acc = proj(0)
    q_hg = acc * _sigmoid(acc)

    acc = proj(1)
    raw = lb_ref[...]
    rows = [raw[r:r + 1, :] for r in range(raw.shape[0])]
    mx = functools.reduce(jnp.maximum, rows)
    ex = [jnp.exp(r - mx) for r in rows]
    den = functools.reduce(lambda a, b: a + b, ex)
    sm = [e / den for e in ex]
    lb = functools.reduce(lambda a, b: a + b, sm[:layer + 1]) - sm[0]
    log_sig = jnp.minimum(acc, 0.0) - jnp.log1p(jnp.exp(-jnp.abs(acc)))
    a = jnp.log(lb)
    c = jnp.log1p(-lb) + log_sig
    log_f = jnp.maximum(a, c) + jnp.log1p(jnp.exp(-jnp.abs(a - c)))
    k_hg = (1.0 - lb) / (1.0 + jnp.exp(acc))
    v_hg = proj(2).astype(BF16)
    gate_hg = _sigmoid(proj(3))

    def hgrn(chunks):
        _hgrn_rows(chunks, q_hg, log_f, k_hg, v_hg, gate_hg,
                   sel_ref, mask_ref, gn_ref, st_ref, ohg_ref)

    n_chunks = tm // HG_CHUNK
    att_refs = (a0_ref, a1_ref, a2_ref)
    n_split = 0
    for grp in range(N_GROUPS):
        dil = ATT_GROUPS[grp][1]
        hgrn(range(grp, min(grp + 1, n_chunks)))
        for which in range(3):
            acc = proj(4 + which * N_GROUPS + grp)
            pieces = [acc[:, hd * ATT_DIM:(hd + 1) * ATT_DIM] for hd in range(ATT_HEADS)]
            if which < 2:
                pieces = [t * rc_ref[which] + pltpu.roll(t, ATT_DIM // 2, 1) * rs_ref[which]
                          for t in pieces]
            if dil == 1:
                for hd in range(ATT_HEADS):
                    c0 = which * ATT_W + hd * ATT_DIM
                    att_refs[grp][0, :, c0:c0 + ATT_DIM] = pieces[hd].astype(BF16)
            else:
                slot = n_split % scr_ref.shape[0]
                n_split += 1
                for hd in range(ATT_HEADS):
                    scr_ref[slot, hd] = pieces[hd]
                for r in range(dil):
                    for hd in range(ATT_HEADS):
                        c0 = which * ATT_W + hd * ATT_DIM
                        att_refs[grp][r, :, c0:c0 + ATT_DIM] = scr_ref[
                            slot, hd, pl.ds(r, tm // dil, stride=dil), :].astype(BF16)

    hgrn(range(min(N_GROUPS, n_chunks), n_chunks))
    for t in range(4):
        gab_ref[:, t * COL_TILE:(t + 1) * COL_TILE] = _sigmoid(proj(13 + t)).astype(BF16)


def _inproj(x2, g1, w_bf, lb_raw, rope_c, rope_s, sel, masks, gn, layer, batch, seq, tm):
    t_rows = x2.shape[0]
    s_tiles = seq // tm
    grid = (t_rows // tm,)
    rope_spec = pl.BlockSpec((2, tm, ATT_DIM), lambda i: (0, i % s_tiles, 0))
    in_specs = [
        pl.BlockSpec((tm, D_MODEL), lambda i: (i, 0)),
        pl.BlockSpec((1, D_MODEL), lambda i: (0, 0)),
        pl.BlockSpec((D_MODEL, N_IN), lambda i: (0, 0), pipeline_mode=pl.Buffered(1)),
        pl.BlockSpec(lb_raw.shape, lambda i: (0, 0)),
        rope_spec, rope_spec,
        pl.BlockSpec(sel.shape, lambda i: (0, 0)),
        pl.BlockSpec(masks.shape, lambda i: (0, 0, 0)),
        pl.BlockSpec((1, HG_W), lambda i: (0, 0)),
    ]
    row = lambda w: pl.BlockSpec((tm, w), lambda i: (i, 0))
    att_spec = lambda dil: pl.BlockSpec(
        (None, dil, tm // dil, 3 * ATT_W), lambda i: (i // s_tiles, 0, i % s_tiles, 0))
    out_specs = [row(HG_W)]
    out_specs += [att_spec(dil) for _, dil in ATT_GROUPS]
    out_specs += [row(2 * D_MODEL)]
    out_shape = [jax.ShapeDtypeStruct((t_rows, HG_W), BF16)]
    out_shape += [jax.ShapeDtypeStruct((batch, dil, seq // dil, 3 * ATT_W), BF16)
                  for _, dil in ATT_GROUPS]
    out_shape += [jax.ShapeDtypeStruct((t_rows, 2 * D_MODEL), BF16)]
    return pl.pallas_call(
        functools.partial(_inproj_body, layer, tm, s_tiles),
        grid=grid, in_specs=in_specs, out_specs=out_specs, out_shape=out_shape,
        scratch_shapes=[pltpu.VMEM((2, ATT_HEADS, tm, ATT_DIM), F32),
                        pltpu.VMEM((HG_HEADS, HG_DIM, HG_DIM), F32)],
        compiler_params=pltpu.CompilerParams(
            dimension_semantics=("arbitrary",), vmem_limit_bytes=VMEM_LIMIT),
        name="inproj",
    )(x2, g1, w_bf, lb_raw, rope_c, rope_s, sel, masks, gn)


def _hgrn_tables():
    c = HG_CHUNK
    t = np.arange(c)[:, None]
    u = np.arange(c)[None, :]
    slabs = [u <= t]
    masks = [np.eye(c, dtype=bool)]
    for lvl in range(HG_LEVELS):
        beta = 1 << lvl
        mid = (t // (2 * beta)) * (2 * beta) + beta - 1
        right = t > mid
        if lvl < HG_SEL_LEVELS:
            slabs.append(np.where(right, (u > mid) & (u <= t), (u > t) & (u <= mid)))
        same = (t // (2 * beta)) == (u // (2 * beta))
        masks.append(same & right & ((u % (2 * beta)) < beta))
    m = np.concatenate(slabs, axis=0).astype(np.float32)
    m3 = np.concatenate([m, m, m], axis=1)
    return jnp.asarray(m3, BF16), jnp.asarray(np.stack(masks).astype(np.float32))


def _hgrn_rows(chunks, q, lf, k, v, gate, sel_ref, mask_ref, gn_ref, st_ref, o_ref):
    c = HG_CHUNK
    rid = lax.broadcasted_iota(jnp.int32, (c, 1), 0)
    row_right = [((rid >> lvl) & 1) == 1 for lvl in range(HG_SEL_LEVELS)]

    for ci in chunks:
        rows = slice(ci * c, (ci + 1) * c)
        g = lf[rows]
        g_hi = g.astype(BF16)
        r1 = g - g_hi.astype(F32)
        g_mid = r1.astype(BF16)
        g_lo = (r1 - g_mid.astype(F32)).astype(BF16)
        g3 = jnp.concatenate([g_hi, g_mid, g_lo], axis=0)
        sums = _dot(sel_ref[...], g3)
        b = sums[0:c]
        e_lvl = [jnp.exp(sums[(lvl + 1) * c:(lvl + 2) * c]) for lvl in range(HG_SEL_LEVELS)]
        for lvl in range(HG_SEL_LEVELS, HG_LEVELS):
            beta = 1 << lvl
            parts = []
            for g0 in range(0, c, 2 * beta):
                b_mid = b[g0 + beta - 1:g0 + beta, :]
                parts.append(b_mid - b[g0:g0 + beta])
                parts.append(b[g0 + beta:g0 + 2 * beta] - b_mid)
            e_lvl.append(jnp.exp(jnp.concatenate(parts, axis=0)))
        e_b = jnp.exp(b)
        e_last = jnp.exp(b[c - 1:c, :] - b)
        for h in range(HG_HEADS):
            hs = slice(h * HG_DIM, (h + 1) * HG_DIM)
            qh = q[rows, hs]
            kh = k[rows, hs]
            vh = v[rows, hs]
            a = mask_ref[0] * _dot_nt(qh.astype(BF16), kh.astype(BF16))
            for lvl in range(HG_SEL_LEVELS):
                x = (jnp.where(row_right[lvl], qh, kh) * e_lvl[lvl][:, hs]).astype(BF16)
                a = a + mask_ref[lvl + 1] * _dot_nt(x, x)
            a_rows = [a[r0:r0 + 8] for r0 in range(0, c, 8)]
            for lvl in range(HG_SEL_LEVELS, HG_LEVELS):
                beta = 1 << lvl
                e = e_lvl[lvl][:, hs]
                right = [r0 for r0 in range(0, c, 8) if (r0 // beta) % 2 == 1]
                ql = jnp.concatenate([qh[r0:r0 + 8] * e[r0:r0 + 8] for r0 in right], axis=0)
                kl = jnp.concatenate(
                    [jnp.zeros((8, HG_DIM), F32) if (r0 // beta) % 2 == 1
                     else kh[r0:r0 + 8] * e[r0:r0 + 8] for r0 in range(0, c, 8)], axis=0)
                p = _dot_nt(ql.astype(BF16), kl.astype(BF16))
                for i, r0 in enumerate(right):
                    a_rows[r0 // 8] = (a_rows[r0 // 8]
                                       + mask_ref[lvl + 1, r0:r0 + 8, :] * p[8 * i:8 * i + 8])
            a = jnp.concatenate(a_rows, axis=0)
            st = st_ref[h]
            o = (_dot(a.astype(BF16), vh)
                 + _dot_nt((qh * e_b[:, hs]).astype(BF16), st.astype(BF16)))
            st_ref[h] = (st * e_b[c - 1:c, hs]
                         + _dot_tn(vh, (kh * e_last[:, hs]).astype(BF16)))
            ms = jnp.mean(o * o, axis=-1, keepdims=True)
            o_ref[rows, hs] = (o * lax.rsqrt(ms + EPS) * gn_ref[:, hs] * gate[rows, hs]).astype(BF16)


def _attn_body(dil, n_blk, res_unroll, q_ref, kc_ref, kp_ref, vc_ref, vp_ref, o_ref, lse_ref, o_scr):
    n = pl.program_id(1)
    blk = ATT_L
    i = lax.broadcasted_iota(jnp.int32, (blk, 2 * blk), 0)
    j = lax.broadcasted_iota(jnp.int32, (blk, 2 * blk), 1)
    band = (j >= i) & (j <= i + blk)
    band_first = band & ((n > 0) | (j >= blk))
    lane = lax.broadcasted_iota(jnp.int32, (blk, ATT_DIM), 1)

    def residue(r, carry):
        for jb in range(n_blk):
            cur = slice(jb * blk, (jb + 1) * blk)
            prev = slice((jb - 1) * blk, jb * blk)
            valid = band_first if jb == 0 else band
            lse_tile = jnp.zeros((blk, ATT_DIM), F32)
            out_rows = pl.ds(jb * blk * dil + r, blk, stride=dil)
            for h in range(ATT_HEADS):
                hs = slice(h * ATT_DIM, (h + 1) * ATT_DIM)
                q = q_ref[r, cur, hs]
                if jb == 0:
                    k_prev, v_prev = kp_ref[r, :, hs], vp_ref[r, :, hs]
                else:
                    k_prev, v_prev = kc_ref[r, prev, hs], vc_ref[r, prev, hs]
                kk = jnp.concatenate([k_prev, kc_ref[r, cur, hs]], axis=0)
                vv = jnp.concatenate([v_prev, vc_ref[r, cur, hs]], axis=0)
                s = jnp.where(valid, _dot_nt(q, kk), NEG_BIG)
                m = jnp.max(s, axis=-1, keepdims=True)
                p = jnp.exp(s - m)
                den = jnp.sum(p, axis=-1, keepdims=True)
                o_scr[h, out_rows, :] = _dot(p.astype(BF16), vv) / den
                lse_tile = jnp.where(lane == h, m + jnp.log(den), lse_tile)
            lse_ref[out_rows, :] = lse_tile
        return carry

    def residues(it, carry):
        for rr in range(res_unroll):
            residue(it * res_unroll + rr, carry)
        return carry

    if dil == res_unroll:
        residues(0, 0)
    else:
        lax.fori_loop(0, dil // res_unroll, residues, 0)
    for h in range(ATT_HEADS):
        o_ref[:, h * ATT_DIM:(h + 1) * ATT_DIM] = o_scr[h].astype(o_ref.dtype)


def _attn_group(qkv_g, grp, batch, seq, n_blk, res_unroll):
    _, dil = ATT_GROUPS[grp]
    m_rows = seq // dil
    steps = m_rows // (ATT_L * n_blk)
    span = ATT_L * n_blk * dil
    cur = lambda c: pl.BlockSpec((None, dil, ATT_L * n_blk, ATT_W), lambda b, n: (b, 0, n, c))
    prev = lambda c: pl.BlockSpec(
        (None, dil, ATT_L, ATT_W), lambda b, n: (b, 0, jnp.maximum(n * n_blk - 1, 0), c))
    in_specs = [cur(0), cur(1), prev(1), cur(2), prev(2)]
    out_specs = [
        pl.BlockSpec((None, span, ATT_W), lambda b, n: (b, n, 0)),
        pl.BlockSpec((None, span, ATT_DIM), lambda b, n: (b, n, 0)),
    ]
    out_shape = [
        jax.ShapeDtypeStruct((batch, seq, ATT_W), BF16),
        jax.ShapeDtypeStruct((batch, seq, ATT_DIM), F32),
    ]
    o, lse = pl.pallas_call(
        functools.partial(_attn_body, dil, n_blk, res_unroll),
        grid=(batch, steps), in_specs=in_specs, out_specs=out_specs, out_shape=out_shape,
        scratch_shapes=[pltpu.VMEM((ATT_HEADS, span, ATT_DIM), F32)],
        compiler_params=pltpu.CompilerParams(
            dimension_semantics=("arbitrary", "arbitrary"), vmem_limit_bytes=VMEM_LIMIT),
        name=f"attn_g{grp}",
    )(qkv_g, qkv_g, qkv_g, qkv_g, qkv_g)
    return o.reshape(batch * seq, ATT_W), lse.reshape(batch * seq, ATT_DIM)


def _merge_body(x_ref, ohg_ref, o0_ref, o1_ref, o2_ref, l0_ref, l1_ref, l2_ref, gab_ref,
                exp_ref, wa_ref, wb_ref, wo_ref, out_ref):
    lses = [l0_ref[...], l1_ref[...], l2_ref[...]]
    mx = jnp.maximum(jnp.maximum(lses[0], lses[1]), lses[2])
    es = [jnp.exp(l - mx) for l in lses]
    inv = 1.0 / (es[0] + es[1] + es[2])
    o_att = None
    for e, o_ref in zip(es, (o0_ref, o1_ref, o2_ref)):
        w = e * inv
        w_hi = w.astype(BF16)
        w_lo = (w - w_hi.astype(F32)).astype(BF16)
        w_full = _dot(jnp.concatenate([w_hi, w_lo], axis=1), exp_ref[...])
        term = w_full * o_ref[...].astype(F32)
        o_att = term if o_att is None else o_att + term
    ga = gab_ref[:, :D_MODEL].astype(F32)
    gb = gab_ref[:, D_MODEL:].astype(F32)
    y = ga * _dot(ohg_ref[...], wa_ref[...]) + gb * _dot(o_att.astype(BF16), wb_ref[...])
    out_ref[...] = x_ref[...] + _dot(y.astype(BF16), wo_ref[...])


def _merge(x2, ohg, outs, lses, gab, expand, wa, wb, wo, tm):
    t_rows = x2.shape[0]
    grid = (t_rows // tm,)
    row = lambda w: pl.BlockSpec((tm, w), lambda i: (i, 0))
    full = lambda a: pl.BlockSpec(a.shape, lambda i: (0, 0))
    in_specs = [row(D_MODEL), row(HG_W), row(ATT_W), row(ATT_W), row(ATT_W),
                row(ATT_DIM), row(ATT_DIM), row(ATT_DIM), row(2 * D_MODEL),
                full(expand), full(wa), full(wb), full(wo)]
    return pl.pallas_call(
        _merge_body, grid=grid, in_specs=in_specs, out_specs=row(D_MODEL),
        out_shape=jax.ShapeDtypeStruct((t_rows, D_MODEL), F32),
        compiler_params=pltpu.CompilerParams(
            dimension_semantics=("arbitrary",), vmem_limit_bytes=VMEM_LIMIT),
        name="merge",
    )(x2, ohg, outs[0], outs[1], outs[2], lses[0], lses[1], lses[2], gab, expand, wa, wb, wo)


def _mlp_body(final, ff_tile, x_ref, g2_ref, wu_ref, wd_ref, gf_ref, out_ref):
    x = x_ref[...]
    ms = jnp.mean(x * x, axis=-1, keepdims=True)
    h = (x * lax.rsqrt(ms + EPS) * g2_ref[...]).astype(BF16)
    acc = x
    for c in range(D_FF // ff_tile):
        cs = slice(c * ff_tile, (c + 1) * ff_tile)
        u = jnp.maximum(_dot(h, wu_ref[:, cs]), 0.0)
        acc = acc + _dot((u * u).astype(BF16), wd_ref[cs, :])
    if final:
        ms2 = jnp.mean(acc * acc, axis=-1, keepdims=True)
        acc = acc * lax.rsqrt(ms2 + EPS) * gf_ref[...]
    out_ref[...] = acc


def _mlp(x2, g2, wu, wd, gf, final, tm):
    t_rows = x2.shape[0]
    grid = (t_rows // tm,)
    row = pl.BlockSpec((tm, D_MODEL), lambda i: (i, 0))
    vec = pl.BlockSpec((1, D_MODEL), lambda i: (0, 0))
    full = lambda a: pl.BlockSpec(a.shape, lambda i: (0, 0))
    return pl.pallas_call(
        functools.partial(_mlp_body, final, 1024),
        grid=grid, in_specs=[row, vec, full(wu), full(wd), vec], out_specs=row,
        out_shape=jax.ShapeDtypeStruct((t_rows, D_MODEL), F32),
        compiler_params=pltpu.CompilerParams(
            dimension_semantics=("arbitrary",), vmem_limit_bytes=VMEM_LIMIT),
        name="mlp",
    )(x2, g2, wu, wd, gf)


def _rope_perm():
    d = np.arange(ATT_DIM)
    half = ATT_DIM // 2
    plain = d[ROPE_DIM:]
    n_first = half - ROPE_HALF
    return np.concatenate([d[:ROPE_HALF], plain[:n_first], d[ROPE_HALF:ROPE_DIM], plain[n_first:]])


def _inproj_col_perm():
    cols = np.arange(N_IN)
    head_perm = _rope_perm()
    base = 4 * HG_W
    for which in range(2):
        for grp in range(N_GROUPS):
            for hd in range(ATT_HEADS):
                c0 = base + (which * N_GROUPS + grp) * ATT_W + hd * ATT_DIM
                cols[c0:c0 + ATT_DIM] = c0 + head_perm
    return cols


def _rope_tables(seq):
    pos = jnp.arange(seq, dtype=F32)
    inv_freq = ROPE_THETA ** (-jnp.arange(0, ROPE_DIM, 2, dtype=F32) / ROPE_DIM)
    ang = pos[:, None] * inv_freq[None, :]
    cos, sin = jnp.cos(ang), jnp.sin(ang)
    n_plain = ATT_DIM // 2 - ROPE_HALF
    ones = jnp.ones((seq, n_plain), F32)
    zeros = jnp.zeros((seq, n_plain), F32)
    c = jnp.concatenate([cos, ones, cos, ones], axis=1)
    s = jnp.concatenate([-sin, zeros, sin, zeros], axis=1)
    scale = ATT_DIM ** -0.5
    stack = lambda t: jnp.stack([t * scale, t], axis=0)
    return stack(c), stack(s)


def kernel(x, norm1_g, w_in, hg_lower_bounds, hg_norm_g, w_branch_a, w_branch_b, w_out, norm2_g,
           w_up, w_down, final_norm_g):
    batch, seq, d = x.shape
    depth = w_in.shape[0]
    t_rows = batch * seq
    tm = 512
    tm_in = 512
    rope_c, rope_s = _rope_tables(seq)
    col_perm = _inproj_col_perm()
    sel, masks = _hgrn_tables()
    head_id = np.arange(ATT_DIM)[:, None]
    col_head = np.arange(ATT_W)[None, :] // ATT_DIM
    expand1 = (head_id == col_head).astype(np.float32)
    expand = jnp.asarray(np.concatenate([expand1, expand1], axis=0), BF16)
    lb_raw = hg_lower_bounds.astype(F32)
    gf = final_norm_g.reshape(1, d).astype(F32)

    x2 = x.reshape(t_rows, d)
    for l in range(depth):
        ohg, a0, a1, a2, gab = _inproj(
            x2, norm1_g[l].reshape(1, d), w_in[l][:, col_perm].astype(BF16), lb_raw, rope_c, rope_s,
            sel, masks, hg_norm_g[l].reshape(1, HG_W), l, batch, seq, tm_in)
        outs, lses = [], []
        for grp, qkv_g in enumerate((a0, a1, a2)):
            o_g, lse_g = _attn_group(qkv_g, grp, batch, seq, ATT_BLOCKS_PER_STEP[grp],
                                     ATT_RESIDUE_UNROLL[grp])
            outs.append(o_g)
            lses.append(lse_g)
        x2 = _merge(x2, ohg, outs, lses, gab, expand, w_branch_a[l].astype(BF16),
                    w_branch_b[l].astype(BF16), w_out[l].astype(BF16), tm)
        x2 = _mlp(x2, norm2_g[l].reshape(1, d), w_up[l].astype(BF16), w_down[l].astype(BF16), gf,
                  l == depth - 1, tm)
    return x2.reshape(batch, seq, d)
```

```python
import functools

import numpy as np
import jax
import jax.numpy as jnp
from jax import lax
from jax.experimental import pallas as pl
from jax.experimental.pallas import tpu as pltpu

F32 = jnp.float32
BF16 = jnp.bfloat16

D_MODEL = 1024
HG_HEADS = 4
HG_DIM = 128
HG_W = HG_HEADS * HG_DIM
ATT_GROUPS = ((128, 1), (512, 4), (2048, 16))
N_GROUPS = len(ATT_GROUPS)
ATT_HEADS = 4
ATT_DIM = 128
ATT_W = ATT_HEADS * ATT_DIM
ATT_QKV_W = 3 * N_GROUPS * ATT_W
ATT_L = 128
ROPE_THETA = 500000.0
ROPE_DIM = ATT_DIM // 4
ROPE_HALF = ROPE_DIM // 2
D_FF = 4 * D_MODEL
EPS = 1e-6
N_IN = 4 * HG_W + ATT_QKV_W + 2 * D_MODEL

COL_TILE = 512
HG_CHUNK = 128
HG_LEVELS = 7
HG_SEL_LEVELS = 3
NEG_BIG = -1e30
ATT_BLOCKS_PER_STEP = (8, 2, 1)
ATT_RESIDUE_UNROLL = (1, 4, 8)
VMEM_LIMIT = 56 * 1024 * 1024


def _sigmoid(z):
    return 0.5 * jnp.tanh(0.5 * z) + 0.5


def _dot(a, b):
    return jnp.dot(a, b, preferred_element_type=F32)


def _dot_nt(a, b):
    return lax.dot_general(a, b, (((1,), (1,)), ((), ())), preferred_element_type=F32)


def _dot_tn(a, b):
    return lax.dot_general(a, b, (((0,), (0,)), ((), ())), preferred_element_type=F32)


def _inproj_body(layer, tm, s_tiles, x_ref, g1_ref, w_ref, lb_ref, rc_ref, rs_ref,
                 sel_ref, mask_ref, gn_ref,
                 ohg_ref, a0_ref, a1_ref, a2_ref, gab_ref, scr_ref, st_ref):
    @pl.when(pl.program_id(0) % s_tiles == 0)
    def _():
        st_ref[...] = jnp.zeros_like(st_ref)

    x = x_ref[...]
    ms = jnp.mean(x * x, axis=-1, keepdims=True)
    h = (x * lax.rsqrt(ms + EPS) * g1_ref[...]).astype(BF16)

    def proj(tile):
        return _dot(h, w_ref[:, tile * COL_TILE:(tile + 1) * COL_TILE])

    acc = proj(0)
    q_hg = acc * _sigmoid(acc)

    acc = proj(1)
    raw = lb_ref[...]
    rows = [raw[r:r + 1, :] for r in range(raw.shape[0])]
    mx = functools.reduce(jnp.maximum, rows)
    ex = [jnp.exp(r - mx) for r in rows]
    den = functools.reduce(lambda a, b: a + b, ex)
    sm = [e / den for e in ex]
    lb = functools.reduce(lambda a, b: a + b, sm[:layer + 1]) - sm[0]
    log_sig = jnp.minimum(acc, 0.0) - jnp.log1p(jnp.exp(-jnp.abs(acc)))
    a = jnp.log(lb)
    c = jnp.log1p(-lb) + log_sig
    log_f = jnp.maximum(a, c) + jnp.log1p(jnp.exp(-jnp.abs(a - c)))
    k_hg = (1.0 - lb) / (1.0 + jnp.exp(acc))
    v_hg = proj(2).astype(BF16)
    gate_hg = _sigmoid(proj(3))

    def hgrn(chunks):
        _hgrn_rows(chunks, q_hg, log_f, k_hg, v_hg, gate_hg,
                   sel_ref, mask_ref, gn_ref, st_ref, ohg_ref)

    n_chunks = tm // HG_CHUNK
    att_refs = (a0_ref, a1_ref, a2_ref)
    n_split = 0
    for grp in range(N_GROUPS):
        dil = ATT_GROUPS[grp][1]
        hgrn(range(grp, min(grp + 1, n_chunks)))
        for which in range(3):
            acc = proj(4 + which * N_GROUPS + grp)
            pieces = [acc[:, hd * ATT_DIM:(hd + 1) * ATT_DIM] for hd in range(ATT_HEADS)]
            if which < 2:
                pieces = [t * rc_ref[which] + pltpu.roll(t, ATT_DIM // 2, 1) * rs_ref[which]
                          for t in pieces]
            if dil == 1:
                for hd in range(ATT_HEADS):
                    c0 = which * ATT_W + hd * ATT_DIM
                    att_refs[grp][0, :, c0:c0 + ATT_DIM] = pieces[hd].astype(BF16)
            else:
                slot = n_split % scr_ref.shape[0]
                n_split += 1
                for hd in range(ATT_HEADS):
                    scr_ref[slot, hd] = pieces[hd]
                for r in range(dil):
                    for hd in range(ATT_HEADS):
                        c0 = which * ATT_W + hd * ATT_DIM
                        att_refs[grp][r, :, c0:c0 + ATT_DIM] = scr_ref[
                            slot, hd, pl.ds(r, tm // dil, stride=dil), :].astype(BF16)

    hgrn(range(min(N_GROUPS, n_chunks), n_chunks))
    for t in range(4):
        gab_ref[:, t * COL_TILE:(t + 1) * COL_TILE] = _sigmoid(proj(13 + t)).astype(BF16)


def _inproj(x2, g1, w_bf, lb_raw, rope_c, rope_s, sel, masks, gn, layer, batch, seq, tm):
    t_rows = x2.shape[0]
    s_tiles = seq // tm
    grid = (t_rows // tm,)
    rope_spec = pl.BlockSpec((2, tm, ATT_DIM), lambda i: (0, i % s_tiles, 0))
    in_specs = [
        pl.BlockSpec((tm, D_MODEL), lambda i: (i, 0)),
        pl.BlockSpec((1, D_MODEL), lambda i: (0, 0)),
        pl.BlockSpec((D_MODEL, N_IN), lambda i: (0, 0), pipeline_mode=pl.Buffered(1)),
        pl.BlockSpec(lb_raw.shape, lambda i: (0, 0)),
        rope_spec, rope_spec,
        pl.BlockSpec(sel.shape, lambda i: (0, 0)),
        pl.BlockSpec(masks.shape, lambda i: (0, 0, 0)),
        pl.BlockSpec((1, HG_W), lambda i: (0, 0)),
    ]
    row = lambda w: pl.BlockSpec((tm, w), lambda i: (i, 0))
    att_spec = lambda dil: pl.BlockSpec(
        (None, dil, tm // dil, 3 * ATT_W), lambda i: (i // s_tiles, 0, i % s_tiles, 0))
    out_specs = [row(HG_W)]
    out_specs += [att_spec(dil) for _, dil in ATT_GROUPS]
    out_specs += [row(2 * D_MODEL)]
    out_shape = [jax.ShapeDtypeStruct((t_rows, HG_W), BF16)]
    out_shape += [jax.ShapeDtypeStruct((batch, dil, seq // dil, 3 * ATT_W), BF16)
                  for _, dil in ATT_GROUPS]
    out_shape += [jax.ShapeDtypeStruct((t_rows, 2 * D_MODEL), BF16)]
    return pl.pallas_call(
        functools.partial(_inproj_body, layer, tm, s_tiles),
        grid=grid, in_specs=in_specs, out_specs=out_specs, out_shape=out_shape,
        scratch_shapes=[pltpu.VMEM((2, ATT_HEADS, tm, ATT_DIM), F32),
                        pltpu.VMEM((HG_HEADS, HG_DIM, HG_DIM), F32)],
        compiler_params=pltpu.CompilerParams(
            dimension_semantics=("arbitrary",), vmem_limit_bytes=VMEM_LIMIT),
        name="inproj",
    )(x2, g1, w_bf, lb_raw, rope_c, rope_s, sel, masks, gn)


def _hgrn_tables():
    c = HG_CHUNK
    t = np.arange(c)[:, None]
    u = np.arange(c)[None, :]
    slabs = [u <= t]
    masks = [np.eye(c, dtype=bool)]
    for lvl in range(HG_LEVELS):
        beta = 1 << lvl
        mid = (t // (2 * beta)) * (2 * beta) + beta - 1
        right = t > mid
        if lvl < HG_SEL_LEVELS:
            slabs.append(np.where(right, (u > mid) & (u <= t), (u > t) & (u <= mid)))
        same = (t // (2 * beta)) == (u // (2 * beta))
        masks.append(same & right & ((u % (2 * beta)) < beta))
    m = np.concatenate(slabs, axis=0).astype(np.float32)
    m3 = np.concatenate([m, m, m], axis=1)
    return jnp.asarray(m3, BF16), jnp.asarray(np.stack(masks).astype(np.float32))


def _hgrn_rows(chunks, q, lf, k, v, gate, sel_ref, mask_ref, gn_ref, st_ref, o_ref):
    c = HG_CHUNK
    rid = lax.broadcasted_iota(jnp.int32, (c, 1), 0)
    row_right = [((rid >> lvl) & 1) == 1 for lvl in range(HG_SEL_LEVELS)]

    for ci in chunks:
        rows = slice(ci * c, (ci + 1) * c)
        g = lf[rows]
        g_hi = g.astype(BF16)
        r1 = g - g_hi.astype(F32)
        g_mid = r1.astype(BF16)
        g_lo = (r1 - g_mid.astype(F32)).astype(BF16)
        g3 = jnp.concatenate([g_hi, g_mid, g_lo], axis=0)
        sums = _dot(sel_ref[...], g3)
        b = sums[0:c]
        e_lvl = [jnp.exp(sums[(lvl + 1) * c:(lvl + 2) * c]) for lvl in range(HG_SEL_LEVELS)]
        for lvl in range(HG_SEL_LEVELS, HG_LEVELS):
            beta = 1 << lvl
            parts = []
            for g0 in range(0, c, 2 * beta):
                b_mid = b[g0 + beta - 1:g0 + beta, :]
                parts.append(b_mid - b[g0:g0 + beta])
                parts.append(b[g0 + beta:g0 + 2 * beta] - b_mid)
            e_lvl.append(jnp.exp(jnp.concatenate(parts, axis=0)))
        e_b = jnp.exp(b)
        e_last = jnp.exp(b[c - 1:c, :] - b)
        for h in range(HG_HEADS):
            hs = slice(h * HG_DIM, (h + 1) * HG_DIM)
            qh = q[rows, hs]
            kh = k[rows, hs]
            vh = v[rows, hs]
            a = mask_ref[0] * _dot_nt(qh.astype(BF16), kh.astype(BF16))
            for lvl in range(HG_SEL_LEVELS):
                x = (jnp.where(row_right[lvl], qh, kh) * e_lvl[lvl][:, hs]).astype(BF16)
                a = a + mask_ref[lvl + 1] * _dot_nt(x, x)
            a_rows = [a[r0:r0 + 8] for r0 in range(0, c, 8)]
            for lvl in range(HG_SEL_LEVELS, HG_LEVELS):
                beta = 1 << lvl
                e = e_lvl[lvl][:, hs]
                right = [r0 for r0 in range(0, c, 8) if (r0 // beta) % 2 == 1]
                ql = jnp.concatenate([qh[r0:r0 + 8] * e[r0:r0 + 8] for r0 in right], axis=0)
                kl = jnp.concatenate(
                    [jnp.zeros((8, HG_DIM), F32) if (r0 // beta) % 2 == 1
                     else kh[r0:r0 + 8] * e[r0:r0 + 8] for r0 in range(0, c, 8)], axis=0)
                p = _dot_nt(ql.astype(BF16), kl.astype(BF16))
                for i, r0 in enumerate(right):
                    a_rows[r0 // 8] = (a_rows[r0 // 8]
                                       + mask_ref[lvl + 1, r0:r0 + 8, :] * p[8 * i:8 * i + 8])
            a = jnp.concatenate(a_rows, axis=0)
            st = st_ref[h]
            o = (_dot(a.astype(BF16), vh)
                 + _dot_nt((qh * e_b[:, hs]).astype(BF16), st.astype(BF16)))
            st_ref[h] = (st * e_b[c - 1:c, hs]
                         + _dot_tn(vh, (kh * e_last[:, hs]).astype(BF16)))
            ms = jnp.mean(o * o, axis=-1, keepdims=True)
            o_ref[rows, hs] = (o * lax.rsqrt(ms + EPS) * gn_ref[:, hs] * gate[rows, hs]).astype(BF16)


def _attn_body(dil, n_blk, res_unroll, q_ref, kc_ref, kp_ref, vc_ref, vp_ref, o_ref, lse_ref, o_scr):
    n = pl.program_id(1)
    blk = ATT_L
    i = lax.broadcasted_iota(jnp.int32, (blk, 2 * blk), 0)
    j = lax.broadcasted_iota(jnp.int32, (blk, 2 * blk), 1)
    band = (j >= i) & (j <= i + blk)
    band_first = band & ((n > 0) | (j >= blk))
    lane = lax.broadcasted_iota(jnp.int32, (blk, ATT_DIM), 1)
    ones = jnp.ones((2 * blk, ATT_DIM), BF16)

    def residue(r, carry):
        for jb in range(n_blk):
            cur = slice(jb * blk, (jb + 1) * blk)
            prev = slice((jb - 1) * blk, jb * blk)
            valid = band_first if jb == 0 else band
            lse_tile = jnp.zeros((blk, ATT_DIM), F32)
            out_rows = pl.ds(jb * blk * dil + r, blk, stride=dil)
            for h in range(ATT_HEADS):
                hs = slice(h * ATT_DIM, (h + 1) * ATT_DIM)
                q = q_ref[r, cur, hs]
                if jb == 0:
                    k_prev, v_prev = kp_ref[r, :, hs], vp_ref[r, :, hs]
                else:
                    k_prev, v_prev = kc_ref[r, prev, hs], vc_ref[r, prev, hs]
                kk = jnp.concatenate([k_prev, kc_ref[r, cur, hs]], axis=0)
                vv = jnp.concatenate([v_prev, vc_ref[r, cur, hs]], axis=0)
                s = jnp.where(valid, _dot_nt(q, kk), NEG_BIG)
                m = jnp.max(s, axis=-1, keepdims=True)
                p = jnp.exp(s - m).astype(BF16)
                ov = _dot(p, jnp.concatenate([vv, ones], axis=1))
                den = ov[:, ATT_DIM:]
                o_scr[h, out_rows, :] = ov[:, :ATT_DIM] / den
                lse_tile = jnp.where(lane == h, m + jnp.log(den), lse_tile)
            lse_ref[out_rows, :] = lse_tile
        return carry

    def residues(it, carry):
        for rr in range(res_unroll):
            residue(it * res_unroll + rr, carry)
        return carry

    if dil == res_unroll:
        residues(0, 0)
    else:
        lax.fori_loop(0, dil // res_unroll, residues, 0)
    for h in range(ATT_HEADS):
        o_ref[:, h * ATT_DIM:(h + 1) * ATT_DIM] = o_scr[h].astype(o_ref.dtype)


def _attn_group(qkv_g, grp, batch, seq, n_blk, res_unroll):
    _, dil = ATT_GROUPS[grp]
    m_rows = seq // dil
    steps = m_rows // (ATT_L * n_blk)
    span = ATT_L * n_blk * dil
    cur = lambda c: pl.BlockSpec((None, dil, ATT_L * n_blk, ATT_W), lambda b, n: (b, 0, n, c))
    prev = lambda c: pl.BlockSpec(
        (None, dil, ATT_L, ATT_W), lambda b, n: (b, 0, jnp.maximum(n * n_blk - 1, 0), c))
    in_specs = [cur(0), cur(1), prev(1), cur(2), prev(2)]
    out_specs = [
        pl.BlockSpec((None, span, ATT_W), lambda b, n: (b, n, 0)),
        pl.BlockSpec((None, span, ATT_DIM), lambda b, n: (b, n, 0)),
    ]
    out_shape = [
        jax.ShapeDtypeStruct((batch, seq, ATT_W), BF16),
        jax.ShapeDtypeStruct((batch, seq, ATT_DIM), F32),
    ]
    o, lse = pl.pallas_call(
        functools.partial(_attn_body, dil, n_blk, res_unroll),
        grid=(batch, steps), in_specs=in_specs, out_specs=out_specs, out_shape=out_shape,
        scratch_shapes=[pltpu.VMEM((ATT_HEADS, span, ATT_DIM), F32)],
        compiler_params=pltpu.CompilerParams(
            dimension_semantics=("arbitrary", "arbitrary"), vmem_limit_bytes=VMEM_LIMIT),
        name=f"attn_g{grp}",
    )(qkv_g, qkv_g, qkv_g, qkv_g, qkv_g)
    return o.reshape(batch * seq, ATT_W), lse.reshape(batch * seq, ATT_DIM)


def _mix_body(final, ff_tile, x_ref, ohg_ref, o0_ref, o1_ref, o2_ref, l0_ref, l1_ref, l2_ref,
              gab_ref, exp_ref, wa_ref, wb_ref, wo_ref, g2_ref, wu_ref, wd_ref, gf_ref, out_ref):
    lses = [l0_ref[...], l1_ref[...], l2_ref[...]]
    mx = jnp.maximum(jnp.maximum(lses[0], lses[1]), lses[2])
    es = [jnp.exp(l - mx) for l in lses]
    inv = 1.0 / (es[0] + es[1] + es[2])
    o_att = None
    for e, o_ref in zip(es, (o0_ref, o1_ref, o2_ref)):
        w = e * inv
        w_hi = w.astype(BF16)
        w_lo = (w - w_hi.astype(F32)).astype(BF16)
        w_full = _dot(jnp.concatenate([w_hi, w_lo], axis=1), exp_ref[...])
        term = w_full * o_ref[...].astype(F32)
        o_att = term if o_att is None else o_att + term
    ga = gab_ref[:, :D_MODEL].astype(F32)
    gb = gab_ref[:, D_MODEL:].astype(F32)
    y = ga * _dot(ohg_ref[...], wa_ref[...]) + gb * _dot(o_att.astype(BF16), wb_ref[...])
    x = x_ref[...] + _dot(y.astype(BF16), wo_ref[...])

    ms = jnp.mean(x * x, axis=-1, keepdims=True)
    h = (x * lax.rsqrt(ms + EPS) * g2_ref[...]).astype(BF16)
    acc = x
    for c in range(D_FF // ff_tile):
        cs = slice(c * ff_tile, (c + 1) * ff_tile)
        u = jnp.maximum(_dot(h, wu_ref[:, cs]), 0.0)
        acc = acc + _dot((u * u).astype(BF16), wd_ref[cs, :])
    if final:
        ms2 = jnp.mean(acc * acc, axis=-1, keepdims=True)
        acc = acc * lax.rsqrt(ms2 + EPS) * gf_ref[...]
    out_ref[...] = acc


def _mix(x2, ohg, outs, lses, gab, expand, wa, wb, wo, g2, wu, wd, gf, final, tm):
    t_rows = x2.shape[0]
    grid = (t_rows // tm,)
    row = lambda w: pl.BlockSpec((tm, w), lambda i: (i, 0))
    full = lambda a: pl.BlockSpec(a.shape, lambda i: (0, 0), pipeline_mode=pl.Buffered(1))
    vec = pl.BlockSpec((1, D_MODEL), lambda i: (0, 0))
    in_specs = [row(D_MODEL), row(HG_W), row(ATT_W), row(ATT_W), row(ATT_W),
                row(ATT_DIM), row(ATT_DIM), row(ATT_DIM), row(2 * D_MODEL),
                full(expand), full(wa), full(wb), full(wo), vec, full(wu), full(wd), vec]
    return pl.pallas_call(
        functools.partial(_mix_body, final, 1024),
        grid=grid, in_specs=in_specs, out_specs=row(D_MODEL),
        out_shape=jax.ShapeDtypeStruct((t_rows, D_MODEL), F32),
        compiler_params=pltpu.CompilerParams(
            dimension_semantics=("arbitrary",), vmem_limit_bytes=VMEM_LIMIT),
        name="mix",
    )(x2, ohg, outs[0], outs[1], outs[2], lses[0], lses[1], lses[2], gab, expand, wa, wb, wo,
      g2, wu, wd, gf)


def _rope_perm():
    d = np.arange(ATT_DIM)
    half = ATT_DIM // 2
    plain = d[ROPE_DIM:]
    n_first = half - ROPE_HALF
    return np.concatenate([d[:ROPE_HALF], plain[:n_first], d[ROPE_HALF:ROPE_DIM], plain[n_first:]])


def _inproj_col_perm():
    cols = np.arange(N_IN)
    head_perm = _rope_perm()
    base = 4 * HG_W
    for which in range(2):
        for grp in range(N_GROUPS):
            for hd in range(ATT_HEADS):
                c0 = base + (which * N_GROUPS + grp) * ATT_W + hd * ATT_DIM
                cols[c0:c0 + ATT_DIM] = c0 + head_perm
    return cols


def _permute_cols(w, cols):
    cuts = [0] + [i for i in range(1, len(cols)) if cols[i] != cols[i - 1] + 1] + [len(cols)]
    runs = [w[:, int(cols[a]):int(cols[b - 1]) + 1] for a, b in zip(cuts[:-1], cuts[1:])]
    return jnp.concatenate(runs, axis=1)


def _rope_tables(seq):
    pos = jnp.arange(seq, dtype=F32)
    inv_freq = ROPE_THETA ** (-jnp.arange(0, ROPE_DIM, 2, dtype=F32) / ROPE_DIM)
    ang = pos[:, None] * inv_freq[None, :]
    cos, sin = jnp.cos(ang), jnp.sin(ang)
    n_plain = ATT_DIM // 2 - ROPE_HALF
    ones = jnp.ones((seq, n_plain), F32)
    zeros = jnp.zeros((seq, n_plain), F32)
    c = jnp.concatenate([cos, ones, cos, ones], axis=1)
    s = jnp.concatenate([-sin, zeros, sin, zeros], axis=1)
    scale = ATT_DIM ** -0.5
    stack = lambda t: jnp.stack([t * scale, t], axis=0)
    return stack(c), stack(s)


def kernel(x, norm1_g, w_in, hg_lower_bounds, hg_norm_g, w_branch_a, w_branch_b, w_out, norm2_g,
           w_up, w_down, final_norm_g):
    batch, seq, d = x.shape
    depth = w_in.shape[0]
    t_rows = batch * seq
    tm = 512
    tm_in = 512
    rope_c, rope_s = _rope_tables(seq)
    col_perm = _inproj_col_perm()
    sel, masks = _hgrn_tables()
    head_id = np.arange(ATT_DIM)[:, None]
    col_head = np.arange(ATT_W)[None, :] // ATT_DIM
    expand1 = (head_id == col_head).astype(np.float32)
    expand = jnp.asarray(np.concatenate([expand1, expand1], axis=0), BF16)
    lb_raw = hg_lower_bounds.astype(F32)
    gf = final_norm_g.reshape(1, d).astype(F32)

    x2 = x.reshape(t_rows, d)
    for l in range(depth):
        ohg, a0, a1, a2, gab = _inproj(
            x2, norm1_g[l].reshape(1, d), _permute_cols(w_in[l], col_perm).astype(BF16), lb_raw, rope_c, rope_s,
            sel, masks, hg_norm_g[l].reshape(1, HG_W), l, batch, seq, tm_in)
        outs, lses = [], []
        for grp, qkv_g in enumerate((a0, a1, a2)):
            o_g, lse_g = _attn_group(qkv_g, grp, batch, seq, ATT_BLOCKS_PER_STEP[grp],
                                     ATT_RESIDUE_UNROLL[grp])
            outs.append(o_g)
            lses.append(lse_g)
        x2 = _mix(x2, ohg, outs, lses, gab, expand, w_branch_a[l].astype(BF16),
                  w_branch_b[l].astype(BF16), w_out[l].astype(BF16), norm2_g[l].reshape(1, d),
                  w_up[l].astype(BF16), w_down[l].astype(BF16), gf, l == depth - 1, tm)
    return x2.reshape(batch, seq, d)
```

```python
import functools

import numpy as np
import jax
import jax.numpy as jnp
from jax import lax
from jax.experimental import pallas as pl
from jax.experimental.pallas import tpu as pltpu

F32 = jnp.float32
BF16 = jnp.bfloat16

D_MODEL = 1024
HG_HEADS = 4
HG_DIM = 128
HG_W = HG_HEADS * HG_DIM
ATT_GROUPS = ((128, 1), (512, 4), (2048, 16))
N_GROUPS = len(ATT_GROUPS)
ATT_HEADS = 4
ATT_DIM = 128
ATT_W = ATT_HEADS * ATT_DIM
ATT_QKV_W = 3 * N_GROUPS * ATT_W
ATT_L = 128
ROPE_THETA = 500000.0
ROPE_DIM = ATT_DIM // 4
ROPE_HALF = ROPE_DIM // 2
D_FF = 4 * D_MODEL
EPS = 1e-6
N_IN = 4 * HG_W + ATT_QKV_W + 2 * D_MODEL

COL_TILE = 512
QK_TILE0 = 4 * HG_W // COL_TILE
VG_TILE0 = QK_TILE0 + 2 * N_GROUPS * ATT_W // COL_TILE
HG_CHUNK = 128
HG_LEVELS = 7
HG_SEL_LEVELS = 3
NEG_BIG = -1e30
ATT_BLOCKS_PER_STEP = (8, 2, 1)
ATT_RESIDUE_UNROLL = (1, 4, 8)
VMEM_LIMIT = 56 * 1024 * 1024


def _sigmoid(z):
    return 0.5 * jnp.tanh(0.5 * z) + 0.5


def _dot(a, b):
    return jnp.dot(a, b, preferred_element_type=F32)


def _dot_nt(a, b):
    return lax.dot_general(a, b, (((1,), (1,)), ((), ())), preferred_element_type=F32)


def _dot_tn(a, b):
    return lax.dot_general(a, b, (((0,), (0,)), ((), ())), preferred_element_type=F32)


def _inproj_body(layer, tm, s_tiles, x_ref, g1_ref, w_hg_ref, w_qk_ref, w_vg_ref, lb_ref,
                 rc_ref, rs_ref, sel_ref, mask_ref, gn_ref,
                 ohg_ref, a0_ref, a1_ref, a2_ref, gab_ref, scr_ref, st_ref):
    @pl.when(pl.program_id(0) % s_tiles == 0)
    def _():
        st_ref[...] = jnp.zeros_like(st_ref)

    x = x_ref[...]
    ms = jnp.mean(x * x, axis=-1, keepdims=True)
    h = (x * lax.rsqrt(ms + EPS) * g1_ref[...]).astype(BF16)

    def proj(tile):
        for w_ref, first in ((w_hg_ref, 0), (w_qk_ref, QK_TILE0), (w_vg_ref, VG_TILE0)):
            n_tiles = w_ref.shape[1] // COL_TILE
            if first <= tile < first + n_tiles:
                c0 = (tile - first) * COL_TILE
                return _dot(h, w_ref[:, c0:c0 + COL_TILE])
        raise ValueError(tile)

    acc = proj(0)
    q_hg = acc * _sigmoid(acc)

    acc = proj(1)
    raw = lb_ref[...]
    rows = [raw[r:r + 1, :] for r in range(raw.shape[0])]
    mx = functools.reduce(jnp.maximum, rows)
    ex = [jnp.exp(r - mx) for r in rows]
    den = functools.reduce(lambda a, b: a + b, ex)
    sm = [e / den for e in ex]
    lb = functools.reduce(lambda a, b: a + b, sm[:layer + 1]) - sm[0]
    log_sig = jnp.minimum(acc, 0.0) - jnp.log1p(jnp.exp(-jnp.abs(acc)))
    a = jnp.log(lb)
    c = jnp.log1p(-lb) + log_sig
    log_f = jnp.maximum(a, c) + jnp.log1p(jnp.exp(-jnp.abs(a - c)))
    k_hg = (1.0 - lb) / (1.0 + jnp.exp(acc))
    v_hg = proj(2).astype(BF16)
    gate_hg = _sigmoid(proj(3))

    def hgrn(chunks):
        _hgrn_rows(chunks, q_hg, log_f, k_hg, v_hg, gate_hg,
                   sel_ref, mask_ref, gn_ref, st_ref, ohg_ref)

    n_chunks = tm // HG_CHUNK
    att_refs = (a0_ref, a1_ref, a2_ref)
    n_split = 0
    for grp in range(N_GROUPS):
        dil = ATT_GROUPS[grp][1]
        hgrn(range(grp, min(grp + 1, n_chunks)))
        for which in range(3):
            acc = proj(4 + which * N_GROUPS + grp)
            pieces = [acc[:, hd * ATT_DIM:(hd + 1) * ATT_DIM] for hd in range(ATT_HEADS)]
            if which < 2:
                pieces = [t * rc_ref[which] + pltpu.roll(t, ATT_DIM // 2, 1) * rs_ref[which]
                          for t in pieces]
            if dil == 1:
                for hd in range(ATT_HEADS):
                    c0 = which * ATT_W + hd * ATT_DIM
                    att_refs[grp][0, :, c0:c0 + ATT_DIM] = pieces[hd].astype(BF16)
            else:
                slot = n_split % scr_ref.shape[0]
                n_split += 1
                for hd in range(ATT_HEADS):
                    scr_ref[slot, hd] = pieces[hd]
                for r in range(dil):
                    for hd in range(ATT_HEADS):
                        c0 = which * ATT_W + hd * ATT_DIM
                        att_refs[grp][r, :, c0:c0 + ATT_DIM] = scr_ref[
                            slot, hd, pl.ds(r, tm // dil, stride=dil), :].astype(BF16)

    hgrn(range(min(N_GROUPS, n_chunks), n_chunks))
    for t in range(4):
        gab_ref[:, t * COL_TILE:(t + 1) * COL_TILE] = _sigmoid(proj(13 + t)).astype(BF16)


def _inproj(x2, g1, w_parts, lb_raw, rope_c, rope_s, sel, masks, gn, layer, batch, seq, tm):
    t_rows = x2.shape[0]
    s_tiles = seq // tm
    grid = (t_rows // tm,)
    rope_spec = pl.BlockSpec((2, tm, ATT_DIM), lambda i: (0, i % s_tiles, 0))
    in_specs = [
        pl.BlockSpec((tm, D_MODEL), lambda i: (i, 0)),
        pl.BlockSpec((1, D_MODEL), lambda i: (0, 0)),
        *[pl.BlockSpec((None,) + w.shape[1:], lambda i: (layer, 0, 0), pipeline_mode=pl.Buffered(1))
          for w in w_parts],
        pl.BlockSpec(lb_raw.shape, lambda i: (0, 0)),
        rope_spec, rope_spec,
        pl.BlockSpec(sel.shape, lambda i: (0, 0)),
        pl.BlockSpec(masks.shape, lambda i: (0, 0, 0)),
        pl.BlockSpec((1, HG_W), lambda i: (0, 0)),
    ]
    row = lambda w: pl.BlockSpec((tm, w), lambda i: (i, 0))
    att_spec = lambda dil: pl.BlockSpec(
        (None, dil, tm // dil, 3 * ATT_W), lambda i: (i // s_tiles, 0, i % s_tiles, 0))
    out_specs = [row(HG_W)]
    out_specs += [att_spec(dil) for _, dil in ATT_GROUPS]
    out_specs += [row(2 * D_MODEL)]
    out_shape = [jax.ShapeDtypeStruct((t_rows, HG_W), BF16)]
    out_shape += [jax.ShapeDtypeStruct((batch, dil, seq // dil, 3 * ATT_W), BF16)
                  for _, dil in ATT_GROUPS]
    out_shape += [jax.ShapeDtypeStruct((t_rows, 2 * D_MODEL), BF16)]
    return pl.pallas_call(
        functools.partial(_inproj_body, layer, tm, s_tiles),
        grid=grid, in_specs=in_specs, out_specs=out_specs, out_shape=out_shape,
        scratch_shapes=[pltpu.VMEM((2, ATT_HEADS, tm, ATT_DIM), F32),
                        pltpu.VMEM((HG_HEADS, HG_DIM, HG_DIM), F32)],
        compiler_params=pltpu.CompilerParams(
            dimension_semantics=("arbitrary",), vmem_limit_bytes=VMEM_LIMIT),
        name="inproj",
    )(x2, g1[layer], *w_parts, lb_raw, rope_c, rope_s, sel, masks, gn[layer])


def _hgrn_tables():
    c = HG_CHUNK
    t = np.arange(c)[:, None]
    u = np.arange(c)[None, :]
    slabs = [u <= t]
    masks = [np.eye(c, dtype=bool)]
    for lvl in range(HG_LEVELS):
        beta = 1 << lvl
        mid = (t // (2 * beta)) * (2 * beta) + beta - 1
        right = t > mid
        if lvl < HG_SEL_LEVELS:
            slabs.append(np.where(right, (u > mid) & (u <= t), (u > t) & (u <= mid)))
        same = (t // (2 * beta)) == (u // (2 * beta))
        masks.append(same & right & ((u % (2 * beta)) < beta))
    m = np.concatenate(slabs, axis=0).astype(np.float32)
    m3 = np.concatenate([m, m, m], axis=1)
    return jnp.asarray(m3, BF16), jnp.asarray(np.stack(masks).astype(np.float32))


def _hgrn_rows(chunks, q, lf, k, v, gate, sel_ref, mask_ref, gn_ref, st_ref, o_ref):
    c = HG_CHUNK
    rid = lax.broadcasted_iota(jnp.int32, (c, 1), 0)
    row_right = [((rid >> lvl) & 1) == 1 for lvl in range(HG_SEL_LEVELS)]

    for ci in chunks:
        rows = slice(ci * c, (ci + 1) * c)
        g = lf[rows]
        g_hi = g.astype(BF16)
        r1 = g - g_hi.astype(F32)
        g_mid = r1.astype(BF16)
        g_lo = (r1 - g_mid.astype(F32)).astype(BF16)
        g3 = jnp.concatenate([g_hi, g_mid, g_lo], axis=0)
        sums = _dot(sel_ref[...], g3)
        b = sums[0:c]
        e_lvl = [jnp.exp(sums[(lvl + 1) * c:(lvl + 2) * c]) for lvl in range(HG_SEL_LEVELS)]
        for lvl in range(HG_SEL_LEVELS, HG_LEVELS):
            beta = 1 << lvl
            parts = []
            for g0 in range(0, c, 2 * beta):
                b_mid = b[g0 + beta - 1:g0 + beta, :]
                parts.append(b_mid - b[g0:g0 + beta])
                parts.append(b[g0 + beta:g0 + 2 * beta] - b_mid)
            e_lvl.append(jnp.exp(jnp.concatenate(parts, axis=0)))
        e_b = jnp.exp(b)
        e_last = jnp.exp(b[c - 1:c, :] - b)
        for h in range(HG_HEADS):
            hs = slice(h * HG_DIM, (h + 1) * HG_DIM)
            qh = q[rows, hs]
            kh = k[rows, hs]
            vh = v[rows, hs]
            a = mask_ref[0] * _dot_nt(qh.astype(BF16), kh.astype(BF16))
            for lvl in range(HG_SEL_LEVELS):
                x = (jnp.where(row_right[lvl], qh, kh) * e_lvl[lvl][:, hs]).astype(BF16)
                a = a + mask_ref[lvl + 1] * _dot_nt(x, x)
            a_rows = [a[r0:r0 + 8] for r0 in range(0, c, 8)]
            for lvl in range(HG_SEL_LEVELS, HG_LEVELS):
                beta = 1 << lvl
                e = e_lvl[lvl][:, hs]
                right = [r0 for r0 in range(0, c, 8) if (r0 // beta) % 2 == 1]
                ql = jnp.concatenate([qh[r0:r0 + 8] * e[r0:r0 + 8] for r0 in right], axis=0)
                kl = jnp.concatenate(
                    [jnp.zeros((8, HG_DIM), F32) if (r0 // beta) % 2 == 1
                     else kh[r0:r0 + 8] * e[r0:r0 + 8] for r0 in range(0, c, 8)], axis=0)
                p = _dot_nt(ql.astype(BF16), kl.astype(BF16))
                for i, r0 in enumerate(right):
                    a_rows[r0 // 8] = (a_rows[r0 // 8]
                                       + mask_ref[lvl + 1, r0:r0 + 8, :] * p[8 * i:8 * i + 8])
            a = jnp.concatenate(a_rows, axis=0)
            st = st_ref[h]
            o = (_dot(a.astype(BF16), vh)
                 + _dot_nt((qh * e_b[:, hs]).astype(BF16), st.astype(BF16)))
            st_ref[h] = (st * e_b[c - 1:c, hs]
                         + _dot_tn(vh, (kh * e_last[:, hs]).astype(BF16)))
            ms = jnp.mean(o * o, axis=-1, keepdims=True)
            o_ref[rows, hs] = (o * lax.rsqrt(ms + EPS) * gn_ref[:, hs] * gate[rows, hs]).astype(BF16)


def _attn_body(dil, n_blk, res_unroll, q_ref, kc_ref, kp_ref, vc_ref, vp_ref, o_ref, lse_ref, o_scr):
    n = pl.program_id(1)
    blk = ATT_L
    i = lax.broadcasted_iota(jnp.int32, (blk, 2 * blk), 0)
    j = lax.broadcasted_iota(jnp.int32, (blk, 2 * blk), 1)
    band = (j >= i) & (j <= i + blk)
    band_first = band & ((n > 0) | (j >= blk))
    lane = lax.broadcasted_iota(jnp.int32, (blk, ATT_DIM), 1)
    ones = jnp.ones((2 * blk, ATT_DIM), BF16)

    def residue(r, carry):
        for jb in range(n_blk):
            cur = slice(jb * blk, (jb + 1) * blk)
            prev = slice((jb - 1) * blk, jb * blk)
            valid = band_first if jb == 0 else band
            lse_tile = jnp.zeros((blk, ATT_DIM), F32)
            out_rows = pl.ds(jb * blk * dil + r, blk, stride=dil)
            for h in range(ATT_HEADS):
                hs = slice(h * ATT_DIM, (h + 1) * ATT_DIM)
                q = q_ref[r, cur, hs]
                if jb == 0:
                    k_prev, v_prev = kp_ref[r, :, hs], vp_ref[r, :, hs]
                else:
                    k_prev, v_prev = kc_ref[r, prev, hs], vc_ref[r, prev, hs]
                kk = jnp.concatenate([k_prev, kc_ref[r, cur, hs]], axis=0)
                vv = jnp.concatenate([v_prev, vc_ref[r, cur, hs]], axis=0)
                s = jnp.where(valid, _dot_nt(q, kk), NEG_BIG)
                m = jnp.max(s, axis=-1, keepdims=True)
                p = jnp.exp(s - m).astype(BF16)
                ov = _dot(p, jnp.concatenate([vv, ones], axis=1))
                den = ov[:, ATT_DIM:]
                o_scr[h, out_rows, :] = ov[:, :ATT_DIM] / den
                lse_tile = jnp.where(lane == h, m + jnp.log(den), lse_tile)
            lse_ref[out_rows, :] = lse_tile
        return carry

    def residues(it, carry):
        for rr in range(res_unroll):
            residue(it * res_unroll + rr, carry)
        return carry

    if dil == res_unroll:
        residues(0, 0)
    else:
        lax.fori_loop(0, dil // res_unroll, residues, 0)
    for h in range(ATT_HEADS):
        o_ref[:, h * ATT_DIM:(h + 1) * ATT_DIM] = o_scr[h].astype(o_ref.dtype)


def _attn_group(qkv_g, grp, batch, seq, n_blk, res_unroll):
    _, dil = ATT_GROUPS[grp]
    m_rows = seq // dil
    steps = m_rows // (ATT_L * n_blk)
    span = ATT_L * n_blk * dil
    cur = lambda c: pl.BlockSpec((None, dil, ATT_L * n_blk, ATT_W), lambda b, n: (b, 0, n, c))
    prev = lambda c: pl.BlockSpec(
        (None, dil, ATT_L, ATT_W), lambda b, n: (b, 0, jnp.maximum(n * n_blk - 1, 0), c))
    in_specs = [cur(0), cur(1), prev(1), cur(2), prev(2)]
    out_specs = [
        pl.BlockSpec((None, span, ATT_W), lambda b, n: (b, n, 0)),
        pl.BlockSpec((None, span, ATT_DIM), lambda b, n: (b, n, 0)),
    ]
    out_shape = [
        jax.ShapeDtypeStruct((batch, seq, ATT_W), BF16),
        jax.ShapeDtypeStruct((batch, seq, ATT_DIM), F32),
    ]
    o, lse = pl.pallas_call(
        functools.partial(_attn_body, dil, n_blk, res_unroll),
        grid=(batch, steps), in_specs=in_specs, out_specs=out_specs, out_shape=out_shape,
        scratch_shapes=[pltpu.VMEM((ATT_HEADS, span, ATT_DIM), F32)],
        compiler_params=pltpu.CompilerParams(
            dimension_semantics=("arbitrary", "arbitrary"), vmem_limit_bytes=VMEM_LIMIT),
        name=f"attn_g{grp}",
    )(qkv_g, qkv_g, qkv_g, qkv_g, qkv_g)
    return o.reshape(batch * seq, ATT_W), lse.reshape(batch * seq, ATT_DIM)


def _mix_body(final, ff_tile, x_ref, ohg_ref, o0_ref, o1_ref, o2_ref, l0_ref, l1_ref, l2_ref,
              gab_ref, exp_ref, wa_ref, wb_ref, wo_ref, g2_ref, wu_ref, wd_ref, gf_ref, out_ref):
    lses = [l0_ref[...], l1_ref[...], l2_ref[...]]
    mx = jnp.maximum(jnp.maximum(lses[0], lses[1]), lses[2])
    es = [jnp.exp(l - mx) for l in lses]
    inv = 1.0 / (es[0] + es[1] + es[2])
    o_att = None
    for e, o_ref in zip(es, (o0_ref, o1_ref, o2_ref)):
        w = e * inv
        w_hi = w.astype(BF16)
        w_lo = (w - w_hi.astype(F32)).astype(BF16)
        w_full = _dot(jnp.concatenate([w_hi, w_lo], axis=1), exp_ref[...])
        term = w_full * o_ref[...].astype(F32)
        o_att = term if o_att is None else o_att + term
    ga = gab_ref[:, :D_MODEL].astype(F32)
    gb = gab_ref[:, D_MODEL:].astype(F32)
    y = ga * _dot(ohg_ref[...], wa_ref[...]) + gb * _dot(o_att.astype(BF16), wb_ref[...])
    x = x_ref[...] + _dot(y.astype(BF16), wo_ref[...])

    ms = jnp.mean(x * x, axis=-1, keepdims=True)
    h = (x * lax.rsqrt(ms + EPS) * g2_ref[...]).astype(BF16)
    acc = x
    for c in range(D_FF // ff_tile):
        cs = slice(c * ff_tile, (c + 1) * ff_tile)
        u = jnp.maximum(_dot(h, wu_ref[:, cs]), 0.0)
        acc = acc + _dot((u * u).astype(BF16), wd_ref[cs, :])
    if final:
        ms2 = jnp.mean(acc * acc, axis=-1, keepdims=True)
        acc = acc * lax.rsqrt(ms2 + EPS) * gf_ref[...]
    out_ref[...] = acc


def _mix(x2, ohg, outs, lses, gab, expand, wa, wb, wo, g2, wu, wd, gf, layer, final, tm):
    t_rows = x2.shape[0]
    grid = (t_rows // tm,)
    row = lambda w: pl.BlockSpec((tm, w), lambda i: (i, 0))
    full = lambda a: pl.BlockSpec(a.shape, lambda i: (0, 0), pipeline_mode=pl.Buffered(1))
    of_layer = lambda a: pl.BlockSpec((None,) + a.shape[1:], lambda i: (layer, 0, 0),
                                      pipeline_mode=pl.Buffered(1))
    vec = pl.BlockSpec((1, D_MODEL), lambda i: (0, 0))
    in_specs = [row(D_MODEL), row(HG_W), row(ATT_W), row(ATT_W), row(ATT_W),
                row(ATT_DIM), row(ATT_DIM), row(ATT_DIM), row(2 * D_MODEL),
                full(expand), of_layer(wa), of_layer(wb), of_layer(wo), vec,
                of_layer(wu), of_layer(wd), vec]
    return pl.pallas_call(
        functools.partial(_mix_body, final, 1024),
        grid=grid, in_specs=in_specs, out_specs=row(D_MODEL),
        out_shape=jax.ShapeDtypeStruct((t_rows, D_MODEL), F32),
        compiler_params=pltpu.CompilerParams(
            dimension_semantics=("arbitrary",), vmem_limit_bytes=VMEM_LIMIT),
        name="mix",
    )(x2, ohg, outs[0], outs[1], outs[2], lses[0], lses[1], lses[2], gab, expand, wa, wb, wo,
      g2[layer], wu, wd, gf)


def _rope_perm():
    d = np.arange(ATT_DIM)
    half = ATT_DIM // 2
    plain = d[ROPE_DIM:]
    n_first = half - ROPE_HALF
    return np.concatenate([d[:ROPE_HALF], plain[:n_first], d[ROPE_HALF:ROPE_DIM], plain[n_first:]])


def _split_w_in(w_in):
    c_qk, c_vg = QK_TILE0 * COL_TILE, VG_TILE0 * COL_TILE
    depth = w_in.shape[0]
    qk = w_in[:, :, c_qk:c_vg].astype(BF16).reshape(depth, D_MODEL, -1, ATT_DIM)
    qk = jnp.take(qk, jnp.asarray(_rope_perm()), axis=3).reshape(depth, D_MODEL, c_vg - c_qk)
    return w_in[:, :, :c_qk].astype(BF16), qk, w_in[:, :, c_vg:].astype(BF16)


def _rope_tables(seq):
    pos = jnp.arange(seq, dtype=F32)
    inv_freq = ROPE_THETA ** (-jnp.arange(0, ROPE_DIM, 2, dtype=F32) / ROPE_DIM)
    ang = pos[:, None] * inv_freq[None, :]
    cos, sin = jnp.cos(ang), jnp.sin(ang)
    n_plain = ATT_DIM // 2 - ROPE_HALF
    ones = jnp.ones((seq, n_plain), F32)
    zeros = jnp.zeros((seq, n_plain), F32)
    c = jnp.concatenate([cos, ones, cos, ones], axis=1)
    s = jnp.concatenate([-sin, zeros, sin, zeros], axis=1)
    scale = ATT_DIM ** -0.5
    stack = lambda t: jnp.stack([t * scale, t], axis=0)
    return stack(c), stack(s)


def kernel(x, norm1_g, w_in, hg_lower_bounds, hg_norm_g, w_branch_a, w_branch_b, w_out, norm2_g,
           w_up, w_down, final_norm_g):
    batch, seq, d = x.shape
    depth = w_in.shape[0]
    t_rows = batch * seq
    tm = 512
    tm_in = 512
    rope_c, rope_s = _rope_tables(seq)
    sel, masks = _hgrn_tables()
    head_id = np.arange(ATT_DIM)[:, None]
    col_head = np.arange(ATT_W)[None, :] // ATT_DIM
    expand1 = (head_id == col_head).astype(np.float32)
    expand = jnp.asarray(np.concatenate([expand1, expand1], axis=0), BF16)
    lb_raw = hg_lower_bounds.astype(F32)
    gf = final_norm_g.reshape(1, d).astype(F32)
    w_parts = _split_w_in(w_in)
    wa, wb, wo = w_branch_a.astype(BF16), w_branch_b.astype(BF16), w_out.astype(BF16)
    wu, wd = w_up.astype(BF16), w_down.astype(BF16)
    g1 = norm1_g.reshape(depth, 1, d)
    g2 = norm2_g.reshape(depth, 1, d)
    gn = hg_norm_g.reshape(depth, 1, HG_W)

    x2 = x.reshape(t_rows, d)
    for l in range(depth):
        ohg, a0, a1, a2, gab = _inproj(x2, g1, w_parts, lb_raw, rope_c, rope_s, sel, masks, gn,
                                       l, batch, seq, tm_in)
        outs, lses = [], []
        for grp, qkv_g in enumerate((a0, a1, a2)):
            o_g, lse_g = _attn_group(qkv_g, grp, batch, seq, ATT_BLOCKS_PER_STEP[grp],
                                     ATT_RESIDUE_UNROLL[grp])
            outs.append(o_g)
            lses.append(lse_g)
        x2 = _mix(x2, ohg, outs, lses, gab, expand, wa, wb, wo, g2, wu, wd, gf,
                  l, l == depth - 1, tm)
    return x2.reshape(batch, seq, d)
```

```python
import functools

import numpy as np
import jax
import jax.numpy as jnp
from jax import lax
from jax.experimental import pallas as pl
from jax.experimental.pallas import tpu as pltpu

F32 = jnp.float32
BF16 = jnp.bfloat16

D_MODEL = 1024
HG_HEADS = 4
HG_DIM = 128
HG_W = HG_HEADS * HG_DIM
ATT_GROUPS = ((128, 1), (512, 4), (2048, 16))
N_GROUPS = len(ATT_GROUPS)
ATT_HEADS = 4
ATT_DIM = 128
ATT_W = ATT_HEADS * ATT_DIM
ATT_QKV_W = 3 * N_GROUPS * ATT_W
ATT_L = 128
ROPE_THETA = 500000.0
ROPE_DIM = ATT_DIM // 4
ROPE_HALF = ROPE_DIM // 2
D_FF = 4 * D_MODEL
EPS = 1e-6
N_IN = 4 * HG_W + ATT_QKV_W + 2 * D_MODEL

COL_TILE = 512
QK_TILE0 = 4 * HG_W // COL_TILE
VG_TILE0 = QK_TILE0 + 2 * N_GROUPS * ATT_W // COL_TILE
HG_CHUNK = 128
HG_LEVELS = 7
HG_SEL_LEVELS = 3
NEG_BIG = -1e30
ATT_BLOCKS_PER_STEP = (16, 4, 1)
ATT_RESIDUE_UNROLL = (1, 4, 8)
VMEM_LIMIT = 56 * 1024 * 1024


def _sigmoid(z):
    return 0.5 * jnp.tanh(0.5 * z) + 0.5


def _dot(a, b):
    return jnp.dot(a, b, preferred_element_type=F32)


def _dot_nt(a, b):
    return lax.dot_general(a, b, (((1,), (1,)), ((), ())), preferred_element_type=F32)


def _dot_tn(a, b):
    return lax.dot_general(a, b, (((0,), (0,)), ((), ())), preferred_element_type=F32)


def _inproj_body(layer, tm, s_tiles, x_ref, g1_ref, w_hg_ref, w_qk_ref, w_vg_ref, lb_ref,
                 rc_ref, rs_ref, sel_ref, mask_ref, gn_ref,
                 ohg_ref, a0_ref, a1_ref, a2_ref, gab_ref, scr_ref, st_ref):
    @pl.when(pl.program_id(0) % s_tiles == 0)
    def _():
        st_ref[...] = jnp.zeros_like(st_ref)

    x = x_ref[...]
    ms = jnp.mean(x * x, axis=-1, keepdims=True)
    h = (x * lax.rsqrt(ms + EPS) * g1_ref[...]).astype(BF16)

    def proj(tile):
        for w_ref, first in ((w_hg_ref, 0), (w_qk_ref, QK_TILE0), (w_vg_ref, VG_TILE0)):
            n_tiles = w_ref.shape[1] // COL_TILE
            if first <= tile < first + n_tiles:
                c0 = (tile - first) * COL_TILE
                return _dot(h, w_ref[:, c0:c0 + COL_TILE])
        raise ValueError(tile)

    acc = proj(0)
    q_hg = acc * _sigmoid(acc)

    acc = proj(1)
    raw = lb_ref[...]
    rows = [raw[r:r + 1, :] for r in range(raw.shape[0])]
    mx = functools.reduce(jnp.maximum, rows)
    ex = [jnp.exp(r - mx) for r in rows]
    den = functools.reduce(lambda a, b: a + b, ex)
    sm = [e / den for e in ex]
    lb = functools.reduce(lambda a, b: a + b, sm[:layer + 1]) - sm[0]
    log_sig = jnp.minimum(acc, 0.0) - jnp.log1p(jnp.exp(-jnp.abs(acc)))
    a = jnp.log(lb)
    c = jnp.log1p(-lb) + log_sig
    log_f = jnp.maximum(a, c) + jnp.log1p(jnp.exp(-jnp.abs(a - c)))
    k_hg = (1.0 - lb) / (1.0 + jnp.exp(acc))
    v_hg = proj(2).astype(BF16)
    gate_hg = _sigmoid(proj(3))

    def hgrn(chunks):
        _hgrn_rows(chunks, q_hg, log_f, k_hg, v_hg, gate_hg,
                   sel_ref, mask_ref, gn_ref, st_ref, ohg_ref)

    n_chunks = tm // HG_CHUNK
    att_refs = (a0_ref, a1_ref, a2_ref)
    n_split = 0
    for grp in range(N_GROUPS):
        dil = ATT_GROUPS[grp][1]
        hgrn(range(grp, min(grp + 1, n_chunks)))
        for which in range(3):
            acc = proj(4 + which * N_GROUPS + grp)
            pieces = [acc[:, hd * ATT_DIM:(hd + 1) * ATT_DIM] for hd in range(ATT_HEADS)]
            if which < 2:
                pieces = [t * rc_ref[which] + pltpu.roll(t, ATT_DIM // 2, 1) * rs_ref[which]
                          for t in pieces]
            if dil == 1:
                for hd in range(ATT_HEADS):
                    c0 = which * ATT_W + hd * ATT_DIM
                    att_refs[grp][0, :, c0:c0 + ATT_DIM] = pieces[hd].astype(BF16)
            else:
                slot = n_split % scr_ref.shape[0]
                n_split += 1
                for hd in range(ATT_HEADS):
                    scr_ref[slot, hd] = pieces[hd]
                for r in range(dil):
                    for hd in range(ATT_HEADS):
                        c0 = which * ATT_W + hd * ATT_DIM
                        att_refs[grp][r, :, c0:c0 + ATT_DIM] = scr_ref[
                            slot, hd, pl.ds(r, tm // dil, stride=dil), :].astype(BF16)

    hgrn(range(min(N_GROUPS, n_chunks), n_chunks))
    for t in range(4):
        gab_ref[:, t * COL_TILE:(t + 1) * COL_TILE] = _sigmoid(proj(13 + t)).astype(BF16)


def _inproj(x2, g1, w_parts, lb_raw, rope_c, rope_s, sel, masks, gn, layer, batch, seq, tm):
    t_rows = x2.shape[0]
    s_tiles = seq // tm
    grid = (t_rows // tm,)
    rope_spec = pl.BlockSpec((2, tm, ATT_DIM), lambda i: (0, i % s_tiles, 0))
    in_specs = [
        pl.BlockSpec((tm, D_MODEL), lambda i: (i, 0)),
        pl.BlockSpec((1, D_MODEL), lambda i: (0, 0)),
        *[pl.BlockSpec((None,) + w.shape[1:], lambda i: (layer, 0, 0), pipeline_mode=pl.Buffered(1))
          for w in w_parts],
        pl.BlockSpec(lb_raw.shape, lambda i: (0, 0)),
        rope_spec, rope_spec,
        pl.BlockSpec(sel.shape, lambda i: (0, 0)),
        pl.BlockSpec(masks.shape, lambda i: (0, 0, 0)),
        pl.BlockSpec((1, HG_W), lambda i: (0, 0)),
    ]
    row = lambda w: pl.BlockSpec((tm, w), lambda i: (i, 0))
    att_spec = lambda dil: pl.BlockSpec(
        (None, dil, tm // dil, 3 * ATT_W), lambda i: (i // s_tiles, 0, i % s_tiles, 0))
    out_specs = [row(HG_W)]
    out_specs += [att_spec(dil) for _, dil in ATT_GROUPS]
    out_specs += [row(2 * D_MODEL)]
    out_shape = [jax.ShapeDtypeStruct((t_rows, HG_W), BF16)]
    out_shape += [jax.ShapeDtypeStruct((batch, dil, seq // dil, 3 * ATT_W), BF16)
                  for _, dil in ATT_GROUPS]
    out_shape += [jax.ShapeDtypeStruct((t_rows, 2 * D_MODEL), BF16)]
    return pl.pallas_call(
        functools.partial(_inproj_body, layer, tm, s_tiles),
        grid=grid, in_specs=in_specs, out_specs=out_specs, out_shape=out_shape,
        scratch_shapes=[pltpu.VMEM((2, ATT_HEADS, tm, ATT_DIM), F32),
                        pltpu.VMEM((HG_HEADS, HG_DIM, HG_DIM), F32)],
        compiler_params=pltpu.CompilerParams(
            dimension_semantics=("arbitrary",), vmem_limit_bytes=VMEM_LIMIT),
        name="inproj",
    )(x2, g1[layer], *w_parts, lb_raw, rope_c, rope_s, sel, masks, gn[layer])


def _hgrn_tables():
    c = HG_CHUNK
    t = np.arange(c)[:, None]
    u = np.arange(c)[None, :]
    slabs = [u <= t]
    masks = [np.eye(c, dtype=bool)]
    for lvl in range(HG_LEVELS):
        beta = 1 << lvl
        mid = (t // (2 * beta)) * (2 * beta) + beta - 1
        right = t > mid
        if 0 < lvl < HG_SEL_LEVELS:
            slabs.append(np.where(right, (u > mid) & (u <= t), (u > t) & (u <= mid)))
        same = (t // (2 * beta)) == (u // (2 * beta))
        masks.append(same & right & ((u % (2 * beta)) < beta))
    m = np.concatenate(slabs, axis=0).astype(np.float32)
    m2 = np.concatenate([m, m], axis=1)
    return jnp.asarray(m2, BF16), jnp.asarray(np.stack(masks).astype(np.float32))


def _hgrn_rows(chunks, q, lf, k, v, gate, sel_ref, mask_ref, gn_ref, st_ref, o_ref):
    c = HG_CHUNK
    rid = lax.broadcasted_iota(jnp.int32, (c, 1), 0)
    row_right = [((rid >> lvl) & 1) == 1 for lvl in range(HG_SEL_LEVELS)]

    for ci in chunks:
        rows = slice(ci * c, (ci + 1) * c)
        g = lf[rows]
        g_hi = g.astype(BF16)
        g_lo = (g - g_hi.astype(F32)).astype(BF16)
        sums = _dot(sel_ref[...], jnp.concatenate([g_hi, g_lo], axis=0))
        b = sums[0:c]
        e_lvl = [jnp.where(row_right[0], jnp.exp(g), 1.0)]
        e_lvl += [jnp.exp(sums[lvl * c:(lvl + 1) * c]) for lvl in range(1, HG_SEL_LEVELS)]
        for lvl in range(HG_SEL_LEVELS, HG_LEVELS):
            beta = 1 << lvl
            parts = []
            for g0 in range(0, c, 2 * beta):
                b_mid = b[g0 + beta - 1:g0 + beta, :]
                parts.append(b_mid - b[g0:g0 + beta])
                parts.append(b[g0 + beta:g0 + 2 * beta] - b_mid)
            e_lvl.append(jnp.exp(jnp.concatenate(parts, axis=0)))
        e_b = jnp.exp(b)
        e_last = jnp.exp(b[c - 1:c, :] - b)
        for h in range(HG_HEADS):
            hs = slice(h * HG_DIM, (h + 1) * HG_DIM)
            qh = q[rows, hs]
            kh = k[rows, hs]
            vh = v[rows, hs]
            a = mask_ref[0] * _dot_nt(qh.astype(BF16), kh.astype(BF16))
            for lvl in range(HG_SEL_LEVELS):
                x = (jnp.where(row_right[lvl], qh, kh) * e_lvl[lvl][:, hs]).astype(BF16)
                a = a + mask_ref[lvl + 1] * _dot_nt(x, x)
            a_rows = [a[r0:r0 + 8] for r0 in range(0, c, 8)]
            for lvl in range(HG_SEL_LEVELS, HG_LEVELS):
                beta = 1 << lvl
                e = e_lvl[lvl][:, hs]
                right = [r0 for r0 in range(0, c, 8) if (r0 // beta) % 2 == 1]
                ql = jnp.concatenate([qh[r0:r0 + 8] * e[r0:r0 + 8] for r0 in right], axis=0)
                kl = jnp.concatenate(
                    [jnp.zeros((8, HG_DIM), F32) if (r0 // beta) % 2 == 1
                     else kh[r0:r0 + 8] * e[r0:r0 + 8] for r0 in range(0, c, 8)], axis=0)
                p = _dot_nt(ql.astype(BF16), kl.astype(BF16))
                for i, r0 in enumerate(right):
                    a_rows[r0 // 8] = (a_rows[r0 // 8]
                                       + mask_ref[lvl + 1, r0:r0 + 8, :] * p[8 * i:8 * i + 8])
            a = jnp.concatenate(a_rows, axis=0)
            st = st_ref[h]
            o = (_dot(a.astype(BF16), vh)
                 + _dot_nt((qh * e_b[:, hs]).astype(BF16), st.astype(BF16)))
            st_ref[h] = (st * e_b[c - 1:c, hs]
                         + _dot_tn(vh, (kh * e_last[:, hs]).astype(BF16)))
            ms = jnp.mean(o * o, axis=-1, keepdims=True)
            o_ref[rows, hs] = (o * lax.rsqrt(ms + EPS) * gn_ref[:, hs] * gate[rows, hs]).astype(BF16)


def _attn_body(dil, n_blk, res_unroll, q_ref, kc_ref, kp_ref, vc_ref, vp_ref, o_ref, lse_ref, o_scr):
    n = pl.program_id(1)
    blk = ATT_L
    i = lax.broadcasted_iota(jnp.int32, (blk, 2 * blk), 0)
    j = lax.broadcasted_iota(jnp.int32, (blk, 2 * blk), 1)
    band = (j >= i) & (j <= i + blk)
    band_first = band & ((n > 0) | (j >= blk))
    lane = lax.broadcasted_iota(jnp.int32, (blk, ATT_DIM), 1)
    ones = jnp.ones((2 * blk, ATT_DIM), BF16)

    def residue(r, carry):
        for jb in range(n_blk):
            cur = slice(jb * blk, (jb + 1) * blk)
            prev = slice((jb - 1) * blk, jb * blk)
            valid = band_first if jb == 0 else band
            lse_tile = jnp.zeros((blk, ATT_DIM), F32)
            out_rows = pl.ds(jb * blk * dil + r, blk, stride=dil)
            for h in range(ATT_HEADS):
                hs = slice(h * ATT_DIM, (h + 1) * ATT_DIM)
                q = q_ref[r, cur, hs]
                if jb == 0:
                    k_prev, v_prev = kp_ref[r, :, hs], vp_ref[r, :, hs]
                else:
                    k_prev, v_prev = kc_ref[r, prev, hs], vc_ref[r, prev, hs]
                kk = jnp.concatenate([k_prev, kc_ref[r, cur, hs]], axis=0)
                vv = jnp.concatenate([v_prev, vc_ref[r, cur, hs]], axis=0)
                s = jnp.where(valid, _dot_nt(q, kk), NEG_BIG)
                m = jnp.max(s, axis=-1, keepdims=True)
                p = jnp.exp(s - m).astype(BF16)
                ov = _dot(p, jnp.concatenate([vv, ones], axis=1))
                den = ov[:, ATT_DIM:]
                o_scr[h, out_rows, :] = ov[:, :ATT_DIM] / den
                lse_tile = jnp.where(lane == h, m + jnp.log(den), lse_tile)
            lse_ref[out_rows, :] = lse_tile
        return carry

    def residues(it, carry):
        for rr in range(res_unroll):
            residue(it * res_unroll + rr, carry)
        return carry

    if dil == res_unroll:
        residues(0, 0)
    else:
        lax.fori_loop(0, dil // res_unroll, residues, 0)
    for h in range(ATT_HEADS):
        o_ref[:, h * ATT_DIM:(h + 1) * ATT_DIM] = o_scr[h].astype(o_ref.dtype)


def _attn_group(qkv_g, grp, batch, seq, n_blk, res_unroll):
    _, dil = ATT_GROUPS[grp]
    m_rows = seq // dil
    steps = m_rows // (ATT_L * n_blk)
    span = ATT_L * n_blk * dil
    cur = lambda c: pl.BlockSpec((None, dil, ATT_L * n_blk, ATT_W), lambda b, n: (b, 0, n, c))
    prev = lambda c: pl.BlockSpec(
        (None, dil, ATT_L, ATT_W), lambda b, n: (b, 0, jnp.maximum(n * n_blk - 1, 0), c))
    in_specs = [cur(0), cur(1), prev(1), cur(2), prev(2)]
    out_specs = [
        pl.BlockSpec((None, span, ATT_W), lambda b, n: (b, n, 0)),
        pl.BlockSpec((None, span, ATT_DIM), lambda b, n: (b, n, 0)),
    ]
    out_shape = [
        jax.ShapeDtypeStruct((batch, seq, ATT_W), BF16),
        jax.ShapeDtypeStruct((batch, seq, ATT_DIM), F32),
    ]
    o, lse = pl.pallas_call(
        functools.partial(_attn_body, dil, n_blk, res_unroll),
        grid=(batch, steps), in_specs=in_specs, out_specs=out_specs, out_shape=out_shape,
        scratch_shapes=[pltpu.VMEM((ATT_HEADS, span, ATT_DIM), F32)],
        compiler_params=pltpu.CompilerParams(
            dimension_semantics=("arbitrary", "arbitrary"), vmem_limit_bytes=VMEM_LIMIT),
        name=f"attn_g{grp}",
    )(qkv_g, qkv_g, qkv_g, qkv_g, qkv_g)
    return o.reshape(batch * seq, ATT_W), lse.reshape(batch * seq, ATT_DIM)


def _mix_body(final, ff_tile, x_ref, ohg_ref, o0_ref, o1_ref, o2_ref, l0_ref, l1_ref, l2_ref,
              gab_ref, exp_ref, wa_ref, wb_ref, wo_ref, g2_ref, wu_ref, wd_ref, gf_ref, out_ref):
    lses = [l0_ref[...], l1_ref[...], l2_ref[...]]
    mx = jnp.maximum(jnp.maximum(lses[0], lses[1]), lses[2])
    es = [jnp.exp(l - mx) for l in lses]
    inv = 1.0 / (es[0] + es[1] + es[2])
    o_att = None
    for e, o_ref in zip(es, (o0_ref, o1_ref, o2_ref)):
        w = e * inv
        w_hi = w.astype(BF16)
        w_lo = (w - w_hi.astype(F32)).astype(BF16)
        w_full = _dot(jnp.concatenate([w_hi, w_lo], axis=1), exp_ref[...])
        term = w_full * o_ref[...].astype(F32)
        o_att = term if o_att is None else o_att + term
    ga = gab_ref[:, :D_MODEL].astype(F32)
    gb = gab_ref[:, D_MODEL:].astype(F32)
    y = ga * _dot(ohg_ref[...], wa_ref[...]) + gb * _dot(o_att.astype(BF16), wb_ref[...])
    x = x_ref[...] + _dot(y.astype(BF16), wo_ref[...])

    ms = jnp.mean(x * x, axis=-1, keepdims=True)
    h = (x * lax.rsqrt(ms + EPS) * g2_ref[...]).astype(BF16)
    acc = x
    for c in range(D_FF // ff_tile):
        cs = slice(c * ff_tile, (c + 1) * ff_tile)
        u = jnp.maximum(_dot(h, wu_ref[:, cs]), 0.0)
        acc = acc + _dot((u * u).astype(BF16), wd_ref[cs, :])
    if final:
        ms2 = jnp.mean(acc * acc, axis=-1, keepdims=True)
        acc = acc * lax.rsqrt(ms2 + EPS) * gf_ref[...]
    out_ref[...] = acc


def _mix(x2, ohg, outs, lses, gab, expand, wa, wb, wo, g2, wu, wd, gf, layer, final, tm):
    t_rows = x2.shape[0]
    grid = (t_rows // tm,)
    row = lambda w: pl.BlockSpec((tm, w), lambda i: (i, 0))
    full = lambda a: pl.BlockSpec(a.shape, lambda i: (0, 0), pipeline_mode=pl.Buffered(1))
    of_layer = lambda a: pl.BlockSpec((None,) + a.shape[1:], lambda i: (layer, 0, 0),
                                      pipeline_mode=pl.Buffered(1))
    vec = pl.BlockSpec((1, D_MODEL), lambda i: (0, 0))
    in_specs = [row(D_MODEL), row(HG_W), row(ATT_W), row(ATT_W), row(ATT_W),
                row(ATT_DIM), row(ATT_DIM), row(ATT_DIM), row(2 * D_MODEL),
                full(expand), of_layer(wa), of_layer(wb), of_layer(wo), vec,
                of_layer(wu), of_layer(wd), vec]
    return pl.pallas_call(
        functools.partial(_mix_body, final, 1024),
        grid=grid, in_specs=in_specs, out_specs=row(D_MODEL),
        out_shape=jax.ShapeDtypeStruct((t_rows, D_MODEL), F32),
        compiler_params=pltpu.CompilerParams(
            dimension_semantics=("arbitrary",), vmem_limit_bytes=VMEM_LIMIT),
        name="mix",
    )(x2, ohg, outs[0], outs[1], outs[2], lses[0], lses[1], lses[2], gab, expand, wa, wb, wo,
      g2[layer], wu, wd, gf)


def _rope_perm():
    d = np.arange(ATT_DIM)
    half = ATT_DIM // 2
    plain = d[ROPE_DIM:]
    n_first = half - ROPE_HALF
    return np.concatenate([d[:ROPE_HALF], plain[:n_first], d[ROPE_HALF:ROPE_DIM], plain[n_first:]])


def _split_w_in(w_in):
    c_qk, c_vg = QK_TILE0 * COL_TILE, VG_TILE0 * COL_TILE
    depth = w_in.shape[0]
    qk = w_in[:, :, c_qk:c_vg].astype(BF16).reshape(depth, D_MODEL, -1, ATT_DIM)
    qk = jnp.take(qk, jnp.asarray(_rope_perm()), axis=3).reshape(depth, D_MODEL, c_vg - c_qk)
    return w_in[:, :, :c_qk].astype(BF16), qk, w_in[:, :, c_vg:].astype(BF16)


def _rope_tables(seq):
    pos = jnp.arange(seq, dtype=F32)
    inv_freq = ROPE_THETA ** (-jnp.arange(0, ROPE_DIM, 2, dtype=F32) / ROPE_DIM)
    ang = pos[:, None] * inv_freq[None, :]
    cos, sin = jnp.cos(ang), jnp.sin(ang)
    n_plain = ATT_DIM // 2 - ROPE_HALF
    ones = jnp.ones((seq, n_plain), F32)
    zeros = jnp.zeros((seq, n_plain), F32)
    c = jnp.concatenate([cos, ones, cos, ones], axis=1)
    s = jnp.concatenate([-sin, zeros, sin, zeros], axis=1)
    scale = ATT_DIM ** -0.5
    stack = lambda t: jnp.stack([t * scale, t], axis=0)
    return stack(c), stack(s)


def kernel(x, norm1_g, w_in, hg_lower_bounds, hg_norm_g, w_branch_a, w_branch_b, w_out, norm2_g,
           w_up, w_down, final_norm_g):
    batch, seq, d = x.shape
    depth = w_in.shape[0]
    t_rows = batch * seq
    tm = 512
    tm_in = 512
    rope_c, rope_s = _rope_tables(seq)
    sel, masks = _hgrn_tables()
    head_id = np.arange(ATT_DIM)[:, None]
    col_head = np.arange(ATT_W)[None, :] // ATT_DIM
    expand1 = (head_id == col_head).astype(np.float32)
    expand = jnp.asarray(np.concatenate([expand1, expand1], axis=0), BF16)
    lb_raw = hg_lower_bounds.astype(F32)
    gf = final_norm_g.reshape(1, d).astype(F32)
    w_parts = _split_w_in(w_in)
    wa, wb, wo = w_branch_a.astype(BF16), w_branch_b.astype(BF16), w_out.astype(BF16)
    wu, wd = w_up.astype(BF16), w_down.astype(BF16)
    g1 = norm1_g.reshape(depth, 1, d)
    g2 = norm2_g.reshape(depth, 1, d)
    gn = hg_norm_g.reshape(depth, 1, HG_W)

    x2 = x.reshape(t_rows, d)
    for l in range(depth):
        ohg, a0, a1, a2, gab = _inproj(x2, g1, w_parts, lb_raw, rope_c, rope_s, sel, masks, gn,
                                       l, batch, seq, tm_in)
        outs, lses = [], []
        for grp, qkv_g in enumerate((a0, a1, a2)):
            o_g, lse_g = _attn_group(qkv_g, grp, batch, seq, ATT_BLOCKS_PER_STEP[grp],
                                     ATT_RESIDUE_UNROLL[grp])
            outs.append(o_g)
            lses.append(lse_g)
        x2 = _mix(x2, ohg, outs, lses, gab, expand, wa, wb, wo, g2, wu, wd, gf,
                  l, l == depth - 1, tm)
    return x2.reshape(batch, seq, d)
```

```python
import functools

import numpy as np
import jax
import jax.numpy as jnp
from jax import lax
from jax.experimental import pallas as pl
from jax.experimental.pallas import tpu as pltpu

F32 = jnp.float32
BF16 = jnp.bfloat16

D_MODEL = 1024
HG_HEADS = 4
HG_DIM = 128
HG_W = HG_HEADS * HG_DIM
ATT_GROUPS = ((128, 1), (512, 4), (2048, 16))
N_GROUPS = len(ATT_GROUPS)
ATT_HEADS = 4
ATT_DIM = 128
ATT_W = ATT_HEADS * ATT_DIM
ATT_QKV_W = 3 * N_GROUPS * ATT_W
ATT_L = 128
ROPE_THETA = 500000.0
ROPE_DIM = ATT_DIM // 4
ROPE_HALF = ROPE_DIM // 2
D_FF = 4 * D_MODEL
EPS = 1e-6
N_IN = 4 * HG_W + ATT_QKV_W + 2 * D_MODEL

COL_TILE = 512
QK_TILE0 = 4 * HG_W // COL_TILE
VG_TILE0 = QK_TILE0 + 2 * N_GROUPS * ATT_W // COL_TILE
HG_CHUNK = 128
HG_LEVELS = 7
HG_SEL_LEVELS = 3
NEG_BIG = -1e30
ATT_BLOCKS_PER_STEP = (16, 4, 1)
ATT_RESIDUE_UNROLL = (1, 4, 16)
VMEM_LIMIT = 56 * 1024 * 1024


def _sigmoid(z):
    return 0.5 * jnp.tanh(0.5 * z) + 0.5


def _dot(a, b):
    return jnp.dot(a, b, preferred_element_type=F32)


def _dot_nt(a, b):
    return lax.dot_general(a, b, (((1,), (1,)), ((), ())), preferred_element_type=F32)


def _dot_tn(a, b):
    return lax.dot_general(a, b, (((0,), (0,)), ((), ())), preferred_element_type=F32)


def _inproj_body(layer, tm, s_tiles, x_ref, g1_ref, w_ref, lb_ref,
                 rc_ref, rs_ref, sel_ref, mask_ref, gn_ref,
                 ohg_ref, a0_ref, a1_ref, a2_ref, gab_ref, scr_ref, st_ref):
    @pl.when(pl.program_id(0) % s_tiles == 0)
    def _():
        st_ref[...] = jnp.zeros_like(st_ref)

    x = x_ref[...]
    ms = jnp.mean(x * x, axis=-1, keepdims=True)
    h = (x * lax.rsqrt(ms + EPS) * g1_ref[...]).astype(BF16)

    def proj(tile):
        return _dot(h, w_ref[:, tile * COL_TILE:(tile + 1) * COL_TILE])

    acc = proj(0)
    q_hg = acc * _sigmoid(acc)

    acc = proj(1)
    raw = lb_ref[...]
    rows = [raw[r:r + 1, :] for r in range(raw.shape[0])]
    mx = functools.reduce(jnp.maximum, rows)
    ex = [jnp.exp(r - mx) for r in rows]
    den = functools.reduce(lambda a, b: a + b, ex)
    sm = [e / den for e in ex]
    lb = functools.reduce(lambda a, b: a + b, sm[:layer + 1]) - sm[0]
    log_sig = jnp.minimum(acc, 0.0) - jnp.log1p(jnp.exp(-jnp.abs(acc)))
    a = jnp.log(lb)
    c = jnp.log1p(-lb) + log_sig
    log_f = jnp.maximum(a, c) + jnp.log1p(jnp.exp(-jnp.abs(a - c)))
    k_hg = (1.0 - lb) / (1.0 + jnp.exp(acc))
    v_hg = proj(2).astype(BF16)
    gate_hg = _sigmoid(proj(3))

    def hgrn(chunks):
        _hgrn_rows(chunks, q_hg, log_f, k_hg, v_hg, gate_hg,
                   sel_ref, mask_ref, gn_ref, st_ref, ohg_ref)

    n_chunks = tm // HG_CHUNK
    att_refs = (a0_ref, a1_ref, a2_ref)
    n_split = 0
    for grp in range(N_GROUPS):
        dil = ATT_GROUPS[grp][1]
        hgrn(range(grp, min(grp + 1, n_chunks)))
        for which in range(3):
            acc = proj(4 + which * N_GROUPS + grp)
            pieces = [acc[:, hd * ATT_DIM:(hd + 1) * ATT_DIM] for hd in range(ATT_HEADS)]
            if which < 2:
                pieces = [t * rc_ref[which] + pltpu.roll(t, ATT_DIM // 2, 1) * rs_ref[which]
                          for t in pieces]
            if dil == 1:
                for hd in range(ATT_HEADS):
                    c0 = which * ATT_W + hd * ATT_DIM
                    att_refs[grp][0, :, c0:c0 + ATT_DIM] = pieces[hd].astype(BF16)
            else:
                slot = n_split % scr_ref.shape[0]
                n_split += 1
                for hd in range(ATT_HEADS):
                    scr_ref[slot, hd] = pieces[hd]
                for r in range(dil):
                    for hd in range(ATT_HEADS):
                        c0 = which * ATT_W + hd * ATT_DIM
                        att_refs[grp][r, :, c0:c0 + ATT_DIM] = scr_ref[
                            slot, hd, pl.ds(r, tm // dil, stride=dil), :].astype(BF16)

    hgrn(range(min(N_GROUPS, n_chunks), n_chunks))
    for t in range(4):
        gab_ref[:, t * COL_TILE:(t + 1) * COL_TILE] = _sigmoid(proj(13 + t)).astype(BF16)


def _inproj(x2, g1, w_bf, lb_raw, rope_c, rope_s, sel, masks, gn, layer, batch, seq, tm):
    t_rows = x2.shape[0]
    s_tiles = seq // tm
    grid = (t_rows // tm,)
    rope_spec = pl.BlockSpec((2, tm, ATT_DIM), lambda i: (0, i % s_tiles, 0))
    in_specs = [
        pl.BlockSpec((tm, D_MODEL), lambda i: (i, 0)),
        pl.BlockSpec((1, D_MODEL), lambda i: (0, 0)),
        pl.BlockSpec((None, D_MODEL, N_IN), lambda i: (layer, 0, 0), pipeline_mode=pl.Buffered(1)),
        pl.BlockSpec(lb_raw.shape, lambda i: (0, 0)),
        rope_spec, rope_spec,
        pl.BlockSpec(sel.shape, lambda i: (0, 0)),
        pl.BlockSpec(masks.shape, lambda i: (0, 0, 0)),
        pl.BlockSpec((1, HG_W), lambda i: (0, 0)),
    ]
    row = lambda w: pl.BlockSpec((tm, w), lambda i: (i, 0))
    att_spec = lambda dil: pl.BlockSpec(
        (None, dil, tm // dil, 3 * ATT_W), lambda i: (i // s_tiles, 0, i % s_tiles, 0))
    out_specs = [row(HG_W)]
    out_specs += [att_spec(dil) for _, dil in ATT_GROUPS]
    out_specs += [row(2 * D_MODEL)]
    out_shape = [jax.ShapeDtypeStruct((t_rows, HG_W), BF16)]
    out_shape += [jax.ShapeDtypeStruct((batch, dil, seq // dil, 3 * ATT_W), BF16)
                  for _, dil in ATT_GROUPS]
    out_shape += [jax.ShapeDtypeStruct((t_rows, 2 * D_MODEL), BF16)]
    return pl.pallas_call(
        functools.partial(_inproj_body, layer, tm, s_tiles),
        grid=grid, in_specs=in_specs, out_specs=out_specs, out_shape=out_shape,
        scratch_shapes=[pltpu.VMEM((2, ATT_HEADS, tm, ATT_DIM), F32),
                        pltpu.VMEM((HG_HEADS, HG_DIM, HG_DIM), F32)],
        compiler_params=pltpu.CompilerParams(
            dimension_semantics=("arbitrary",), vmem_limit_bytes=VMEM_LIMIT),
        name="inproj",
    )(x2, g1[layer], w_bf, lb_raw, rope_c, rope_s, sel, masks, gn[layer])


def _hgrn_tables():
    c = HG_CHUNK
    t = np.arange(c)[:, None]
    u = np.arange(c)[None, :]
    slabs = [u <= t]
    masks = [np.eye(c, dtype=bool)]
    for lvl in range(HG_LEVELS):
        beta = 1 << lvl
        mid = (t // (2 * beta)) * (2 * beta) + beta - 1
        right = t > mid
        if 0 < lvl < HG_SEL_LEVELS:
            slabs.append(np.where(right, (u > mid) & (u <= t), (u > t) & (u <= mid)))
        same = (t // (2 * beta)) == (u // (2 * beta))
        masks.append(same & right & ((u % (2 * beta)) < beta))
    m = np.concatenate(slabs, axis=0).astype(np.float32)
    m2 = np.concatenate([m, m], axis=1)
    return jnp.asarray(m2, BF16), jnp.asarray(np.stack(masks).astype(np.float32))


def _hgrn_rows(chunks, q, lf, k, v, gate, sel_ref, mask_ref, gn_ref, st_ref, o_ref):
    c = HG_CHUNK
    rid = lax.broadcasted_iota(jnp.int32, (c, 1), 0)
    row_right = [((rid >> lvl) & 1) == 1 for lvl in range(HG_SEL_LEVELS)]

    for ci in chunks:
        rows = slice(ci * c, (ci + 1) * c)
        g = lf[rows]
        g_hi = g.astype(BF16)
        g_lo = (g - g_hi.astype(F32)).astype(BF16)
        sums = _dot(sel_ref[...], jnp.concatenate([g_hi, g_lo], axis=0))
        b = sums[0:c]
        e_lvl = [jnp.where(row_right[0], jnp.exp(g), 1.0)]
        e_lvl += [jnp.exp(sums[lvl * c:(lvl + 1) * c]) for lvl in range(1, HG_SEL_LEVELS)]
        for lvl in range(HG_SEL_LEVELS, HG_LEVELS):
            beta = 1 << lvl
            parts = []
            for g0 in range(0, c, 2 * beta):
                b_mid = b[g0 + beta - 1:g0 + beta, :]
                parts.append(b_mid - b[g0:g0 + beta])
                parts.append(b[g0 + beta:g0 + 2 * beta] - b_mid)
            e_lvl.append(jnp.exp(jnp.concatenate(parts, axis=0)))
        e_b = jnp.exp(b)
        e_last = jnp.exp(b[c - 1:c, :] - b)
        for h in range(HG_HEADS):
            hs = slice(h * HG_DIM, (h + 1) * HG_DIM)
            qh = q[rows, hs]
            kh = k[rows, hs]
            vh = v[rows, hs]
            a = mask_ref[0] * _dot_nt(qh.astype(BF16), kh.astype(BF16))
            for lvl in range(HG_SEL_LEVELS):
                x = (jnp.where(row_right[lvl], qh, kh) * e_lvl[lvl][:, hs]).astype(BF16)
                a = a + mask_ref[lvl + 1] * _dot_nt(x, x)
            a_rows = [a[r0:r0 + 8] for r0 in range(0, c, 8)]
            for lvl in range(HG_SEL_LEVELS, HG_LEVELS):
                beta = 1 << lvl
                e = e_lvl[lvl][:, hs]
                right = [r0 for r0 in range(0, c, 8) if (r0 // beta) % 2 == 1]
                ql = jnp.concatenate([qh[r0:r0 + 8] * e[r0:r0 + 8] for r0 in right], axis=0)
                kl = jnp.concatenate(
                    [jnp.zeros((8, HG_DIM), F32) if (r0 // beta) % 2 == 1
                     else kh[r0:r0 + 8] * e[r0:r0 + 8] for r0 in range(0, c, 8)], axis=0)
                p = _dot_nt(ql.astype(BF16), kl.astype(BF16))
                for i, r0 in enumerate(right):
                    a_rows[r0 // 8] = (a_rows[r0 // 8]
                                       + mask_ref[lvl + 1, r0:r0 + 8, :] * p[8 * i:8 * i + 8])
            a = jnp.concatenate(a_rows, axis=0)
            st = st_ref[h]
            o = (_dot(a.astype(BF16), vh)
                 + _dot_nt((qh * e_b[:, hs]).astype(BF16), st.astype(BF16)))
            st_ref[h] = (st * e_b[c - 1:c, hs]
                         + _dot_tn(vh, (kh * e_last[:, hs]).astype(BF16)))
            ms = jnp.mean(o * o, axis=-1, keepdims=True)
            o_ref[rows, hs] = (o * lax.rsqrt(ms + EPS) * gn_ref[:, hs] * gate[rows, hs]).astype(BF16)


def _attn_body(dil, n_blk, res_unroll, q_ref, kc_ref, kp_ref, vc_ref, vp_ref, o_ref, lse_ref, o_scr):
    n = pl.program_id(1)
    blk = ATT_L
    i = lax.broadcasted_iota(jnp.int32, (blk, 2 * blk), 0)
    j = lax.broadcasted_iota(jnp.int32, (blk, 2 * blk), 1)
    band = (j >= i) & (j <= i + blk)
    band_first = band & ((n > 0) | (j >= blk))
    lane = lax.broadcasted_iota(jnp.int32, (blk, ATT_DIM), 1)
    ones = jnp.ones((2 * blk, ATT_DIM), BF16)

    def residue(r, carry):
        for jb in range(n_blk):
            cur = slice(jb * blk, (jb + 1) * blk)
            prev = slice((jb - 1) * blk, jb * blk)
            valid = band_first if jb == 0 else band
            lse_tile = jnp.zeros((blk, ATT_DIM), F32)
            out_rows = pl.ds(jb * blk * dil + r, blk, stride=dil)
            for h in range(ATT_HEADS):
                hs = slice(h * ATT_DIM, (h + 1) * ATT_DIM)
                q = q_ref[r, cur, hs]
                if jb == 0:
                    k_prev, v_prev = kp_ref[r, :, hs], vp_ref[r, :, hs]
                else:
                    k_prev, v_prev = kc_ref[r, prev, hs], vc_ref[r, prev, hs]
                kk = jnp.concatenate([k_prev, kc_ref[r, cur, hs]], axis=0)
                vv = jnp.concatenate([v_prev, vc_ref[r, cur, hs]], axis=0)
                s = jnp.where(valid, _dot_nt(q, kk), NEG_BIG)
                m = jnp.max(s, axis=-1, keepdims=True)
                p = jnp.exp(s - m).astype(BF16)
                ov = _dot(p, jnp.concatenate([vv, ones], axis=1))
                den = ov[:, ATT_DIM:]
                o_scr[h, out_rows, :] = ov[:, :ATT_DIM] / den
                lse_tile = jnp.where(lane == h, m + jnp.log(den), lse_tile)
            lse_ref[out_rows, :] = lse_tile
        return carry

    def residues(it, carry):
        for rr in range(res_unroll):
            residue(it * res_unroll + rr, carry)
        return carry

    if dil == res_unroll:
        residues(0, 0)
    else:
        lax.fori_loop(0, dil // res_unroll, residues, 0)
    for h in range(ATT_HEADS):
        o_ref[:, h * ATT_DIM:(h + 1) * ATT_DIM] = o_scr[h].astype(o_ref.dtype)


def _attn_group(qkv_g, grp, batch, seq, n_blk, res_unroll):
    _, dil = ATT_GROUPS[grp]
    m_rows = seq // dil
    steps = m_rows // (ATT_L * n_blk)
    span = ATT_L * n_blk * dil
    cur = lambda c: pl.BlockSpec((None, dil, ATT_L * n_blk, ATT_W), lambda b, n: (b, 0, n, c))
    prev = lambda c: pl.BlockSpec(
        (None, dil, ATT_L, ATT_W), lambda b, n: (b, 0, jnp.maximum(n * n_blk - 1, 0), c))
    in_specs = [cur(0), cur(1), prev(1), cur(2), prev(2)]
    out_specs = [
        pl.BlockSpec((None, span, ATT_W), lambda b, n: (b, n, 0)),
        pl.BlockSpec((None, span, ATT_DIM), lambda b, n: (b, n, 0)),
    ]
    out_shape = [
        jax.ShapeDtypeStruct((batch, seq, ATT_W), BF16),
        jax.ShapeDtypeStruct((batch, seq, ATT_DIM), F32),
    ]
    o, lse = pl.pallas_call(
        functools.partial(_attn_body, dil, n_blk, res_unroll),
        grid=(batch, steps), in_specs=in_specs, out_specs=out_specs, out_shape=out_shape,
        scratch_shapes=[pltpu.VMEM((ATT_HEADS, span, ATT_DIM), F32)],
        compiler_params=pltpu.CompilerParams(
            dimension_semantics=("arbitrary", "arbitrary"), vmem_limit_bytes=VMEM_LIMIT),
        name=f"attn_g{grp}",
    )(qkv_g, qkv_g, qkv_g, qkv_g, qkv_g)
    return o.reshape(batch * seq, ATT_W), lse.reshape(batch * seq, ATT_DIM)


def _mix_body(final, ff_tile, x_ref, ohg_ref, o0_ref, o1_ref, o2_ref, l0_ref, l1_ref, l2_ref,
              gab_ref, exp_ref, wa_ref, wb_ref, wo_ref, g2_ref, wu_ref, wd_ref, gf_ref, out_ref):
    lses = [l0_ref[...], l1_ref[...], l2_ref[...]]
    mx = jnp.maximum(jnp.maximum(lses[0], lses[1]), lses[2])
    es = [jnp.exp(l - mx) for l in lses]
    inv = 1.0 / (es[0] + es[1] + es[2])
    o_att = None
    for e, o_ref in zip(es, (o0_ref, o1_ref, o2_ref)):
        w = e * inv
        w_hi = w.astype(BF16)
        w_lo = (w - w_hi.astype(F32)).astype(BF16)
        w_full = _dot(jnp.concatenate([w_hi, w_lo], axis=1), exp_ref[...])
        term = w_full * o_ref[...].astype(F32)
        o_att = term if o_att is None else o_att + term
    ga = gab_ref[:, :D_MODEL].astype(F32)
    gb = gab_ref[:, D_MODEL:].astype(F32)
    y = ga * _dot(ohg_ref[...], wa_ref[...]) + gb * _dot(o_att.astype(BF16), wb_ref[...])
    x = x_ref[...] + _dot(y.astype(BF16), wo_ref[...])

    ms = jnp.mean(x * x, axis=-1, keepdims=True)
    h = (x * lax.rsqrt(ms + EPS) * g2_ref[...]).astype(BF16)
    acc = x
    for c in range(D_FF // ff_tile):
        cs = slice(c * ff_tile, (c + 1) * ff_tile)
        u = jnp.maximum(_dot(h, wu_ref[:, cs]), 0.0)
        acc = acc + _dot((u * u).astype(BF16), wd_ref[cs, :])
    if final:
        ms2 = jnp.mean(acc * acc, axis=-1, keepdims=True)
        acc = acc * lax.rsqrt(ms2 + EPS) * gf_ref[...]
    out_ref[...] = acc


def _mix(x2, ohg, outs, lses, gab, expand, wa, wb, wo, g2, wu, wd, gf, layer, final, tm):
    t_rows = x2.shape[0]
    grid = (t_rows // tm,)
    row = lambda w: pl.BlockSpec((tm, w), lambda i: (i, 0))
    full = lambda a: pl.BlockSpec(a.shape, lambda i: (0, 0), pipeline_mode=pl.Buffered(1))
    of_layer = lambda a: pl.BlockSpec((None,) + a.shape[1:], lambda i: (layer, 0, 0),
                                      pipeline_mode=pl.Buffered(1))
    vec = pl.BlockSpec((1, D_MODEL), lambda i: (0, 0))
    in_specs = [row(D_MODEL), row(HG_W), row(ATT_W), row(ATT_W), row(ATT_W),
                row(ATT_DIM), row(ATT_DIM), row(ATT_DIM), row(2 * D_MODEL),
                full(expand), of_layer(wa), of_layer(wb), of_layer(wo), vec,
                of_layer(wu), of_layer(wd), vec]
    return pl.pallas_call(
        functools.partial(_mix_body, final, 1024),
        grid=grid, in_specs=in_specs, out_specs=row(D_MODEL),
        out_shape=jax.ShapeDtypeStruct((t_rows, D_MODEL), F32),
        compiler_params=pltpu.CompilerParams(
            dimension_semantics=("arbitrary",), vmem_limit_bytes=VMEM_LIMIT),
        name="mix",
    )(x2, ohg, outs[0], outs[1], outs[2], lses[0], lses[1], lses[2], gab, expand, wa, wb, wo,
      g2[layer], wu, wd, gf)


def _rope_perm():
    d = np.arange(ATT_DIM)
    half = ATT_DIM // 2
    plain = d[ROPE_DIM:]
    n_first = half - ROPE_HALF
    return np.concatenate([d[:ROPE_HALF], plain[:n_first], d[ROPE_HALF:ROPE_DIM], plain[n_first:]])


def _prep_w_in_body(w_ref, o_ref):
    j = pl.program_id(1)
    is_qk = jnp.logical_and(j >= QK_TILE0, j < VG_TILE0)

    @pl.when(is_qk)
    def _():
        half = ATT_DIM // 2
        for hd in range(ATT_HEADS):
            t = w_ref[:, hd * ATT_DIM:(hd + 1) * ATT_DIM]
            t = jnp.concatenate([t[:, :ROPE_HALF], t[:, ROPE_DIM:ROPE_HALF + half],
                                 t[:, ROPE_HALF:ROPE_DIM], t[:, ROPE_HALF + half:]], axis=1)
            o_ref[:, hd * ATT_DIM:(hd + 1) * ATT_DIM] = t.astype(BF16)

    @pl.when(jnp.logical_not(is_qk))
    def _():
        o_ref[...] = w_ref[...].astype(BF16)


def _prep_w_in(w_in):
    depth = w_in.shape[0]
    spec = pl.BlockSpec((None, D_MODEL, COL_TILE), lambda l, j: (l, 0, j))
    return pl.pallas_call(
        _prep_w_in_body, grid=(depth, N_IN // COL_TILE), in_specs=[spec], out_specs=spec,
        out_shape=jax.ShapeDtypeStruct(w_in.shape, BF16),
        compiler_params=pltpu.CompilerParams(
            dimension_semantics=("arbitrary", "arbitrary"), vmem_limit_bytes=VMEM_LIMIT),
        name="prep_w_in",
    )(w_in)


def _rope_tables(seq):
    pos = jnp.arange(seq, dtype=F32)
    inv_freq = ROPE_THETA ** (-jnp.arange(0, ROPE_DIM, 2, dtype=F32) / ROPE_DIM)
    ang = pos[:, None] * inv_freq[None, :]
    cos, sin = jnp.cos(ang), jnp.sin(ang)
    n_plain = ATT_DIM // 2 - ROPE_HALF
    ones = jnp.ones((seq, n_plain), F32)
    zeros = jnp.zeros((seq, n_plain), F32)
    c = jnp.concatenate([cos, ones, cos, ones], axis=1)
    s = jnp.concatenate([-sin, zeros, sin, zeros], axis=1)
    scale = ATT_DIM ** -0.5
    stack = lambda t: jnp.stack([t * scale, t], axis=0)
    return stack(c), stack(s)


def kernel(x, norm1_g, w_in, hg_lower_bounds, hg_norm_g, w_branch_a, w_branch_b, w_out, norm2_g,
           w_up, w_down, final_norm_g):
    batch, seq, d = x.shape
    depth = w_in.shape[0]
    t_rows = batch * seq
    tm = 512
    tm_in = 512
    rope_c, rope_s = _rope_tables(seq)
    sel, masks = _hgrn_tables()
    head_id = np.arange(ATT_DIM)[:, None]
    col_head = np.arange(ATT_W)[None, :] // ATT_DIM
    expand1 = (head_id == col_head).astype(np.float32)
    expand = jnp.asarray(np.concatenate([expand1, expand1], axis=0), BF16)
    lb_raw = hg_lower_bounds.astype(F32)
    gf = final_norm_g.reshape(1, d).astype(F32)
    w_bf = _prep_w_in(w_in)
    wa, wb, wo = w_branch_a.astype(BF16), w_branch_b.astype(BF16), w_out.astype(BF16)
    wu, wd = w_up.astype(BF16), w_down.astype(BF16)
    g1 = norm1_g.reshape(depth, 1, d)
    g2 = norm2_g.reshape(depth, 1, d)
    gn = hg_norm_g.reshape(depth, 1, HG_W)

    x2 = x.reshape(t_rows, d)
    for l in range(depth):
        ohg, a0, a1, a2, gab = _inproj(x2, g1, w_bf, lb_raw, rope_c, rope_s, sel, masks, gn,
                                       l, batch, seq, tm_in)
        outs, lses = [], []
        for grp, qkv_g in enumerate((a0, a1, a2)):
            o_g, lse_g = _attn_group(qkv_g, grp, batch, seq, ATT_BLOCKS_PER_STEP[grp],
                                     ATT_RESIDUE_UNROLL[grp])
            outs.append(o_g)
            lses.append(lse_g)
        x2 = _mix(x2, ohg, outs, lses, gab, expand, wa, wb, wo, g2, wu, wd, gf,
                  l, l == depth - 1, tm)
    return x2.reshape(batch, seq, d)
```

```python
import functools

import numpy as np
import jax
import jax.numpy as jnp
from jax import lax
from jax.experimental import pallas as pl
from jax.experimental.pallas import tpu as pltpu

F32 = jnp.float32
BF16 = jnp.bfloat16

D_MODEL = 1024
HG_HEADS = 4
HG_DIM = 128
HG_W = HG_HEADS * HG_DIM
ATT_GROUPS = ((128, 1), (512, 4), (2048, 16))
N_GROUPS = len(ATT_GROUPS)
ATT_HEADS = 4
ATT_DIM = 128
ATT_W = ATT_HEADS * ATT_DIM
ATT_QKV_W = 3 * N_GROUPS * ATT_W
ATT_L = 128
ROPE_THETA = 500000.0
ROPE_DIM = ATT_DIM // 4
ROPE_HALF = ROPE_DIM // 2
D_FF = 4 * D_MODEL
EPS = 1e-6
N_IN = 4 * HG_W + ATT_QKV_W + 2 * D_MODEL

COL_TILE = 512
QK_TILE0 = 4 * HG_W // COL_TILE
VG_TILE0 = QK_TILE0 + 2 * N_GROUPS * ATT_W // COL_TILE
HG_CHUNK = 128
HG_LEVELS = 7
HG_SEL_LEVELS = 3
NEG_BIG = -1e30
ATT_BLOCKS_PER_STEP = (16, 4, 1)
ATT_RESIDUE_UNROLL = (1, 4, 16)
VMEM_LIMIT = 56 * 1024 * 1024


def _sigmoid(z):
    return 0.5 * jnp.tanh(0.5 * z) + 0.5


def _dot(a, b):
    return jnp.dot(a, b, preferred_element_type=F32)


def _dot_nt(a, b):
    return lax.dot_general(a, b, (((1,), (1,)), ((), ())), preferred_element_type=F32)


def _dot_tn(a, b):
    return lax.dot_general(a, b, (((0,), (0,)), ((), ())), preferred_element_type=F32)


def _inproj_body(layer, tm, s_tiles, x_ref, g1_ref, w_ref, lb_ref,
                 rc_ref, rs_ref, sel_ref, mask_ref, gn_ref,
                 ohg_ref, a0_ref, a1_ref, a2_ref, gab_ref, scr_ref, st_ref):
    @pl.when(pl.program_id(0) % s_tiles == 0)
    def _():
        st_ref[...] = jnp.zeros_like(st_ref)

    x = x_ref[...]
    ms = jnp.mean(x * x, axis=-1, keepdims=True)
    h = (x * lax.rsqrt(ms + EPS) * g1_ref[...]).astype(BF16)

    def proj(tile):
        return _dot(h, w_ref[:, tile * COL_TILE:(tile + 1) * COL_TILE])

    acc = proj(0)
    q_hg = acc * _sigmoid(acc)

    acc = proj(1)
    raw = lb_ref[...]
    rows = [raw[r:r + 1, :] for r in range(raw.shape[0])]
    mx = functools.reduce(jnp.maximum, rows)
    ex = [jnp.exp(r - mx) for r in rows]
    den = functools.reduce(lambda a, b: a + b, ex)
    sm = [e / den for e in ex]
    lb = functools.reduce(lambda a, b: a + b, sm[:layer + 1]) - sm[0]
    log_sig = jnp.minimum(acc, 0.0) - jnp.log1p(jnp.exp(-jnp.abs(acc)))
    a = jnp.log(lb)
    c = jnp.log1p(-lb) + log_sig
    log_f = jnp.maximum(a, c) + jnp.log1p(jnp.exp(-jnp.abs(a - c)))
    k_hg = (1.0 - lb) / (1.0 + jnp.exp(acc))
    v_hg = proj(2).astype(BF16)
    gate_hg = _sigmoid(proj(3))

    def hgrn(chunks):
        _hgrn_rows(chunks, q_hg, log_f, k_hg, v_hg, gate_hg,
                   sel_ref, mask_ref, gn_ref, st_ref, ohg_ref)

    n_chunks = tm // HG_CHUNK
    att_refs = (a0_ref, a1_ref, a2_ref)
    n_split = 0
    for grp in range(N_GROUPS):
        dil = ATT_GROUPS[grp][1]
        hgrn(range(grp, min(grp + 1, n_chunks)))
        for which in range(3):
            acc = proj(4 + which * N_GROUPS + grp)
            pieces = [acc[:, hd * ATT_DIM:(hd + 1) * ATT_DIM] for hd in range(ATT_HEADS)]
            if which < 2:
                pieces = [t * rc_ref[which] + pltpu.roll(t, ATT_DIM // 2, 1) * rs_ref[which]
                          for t in pieces]
            if dil == 1:
                for hd in range(ATT_HEADS):
                    c0 = which * ATT_W + hd * ATT_DIM
                    att_refs[grp][0, :, c0:c0 + ATT_DIM] = pieces[hd].astype(BF16)
            else:
                slot = n_split % scr_ref.shape[0]
                n_split += 1
                for hd in range(ATT_HEADS):
                    scr_ref[slot, hd] = pieces[hd]
                for r in range(dil):
                    for hd in range(ATT_HEADS):
                        c0 = which * ATT_W + hd * ATT_DIM
                        att_refs[grp][r, :, c0:c0 + ATT_DIM] = scr_ref[
                            slot, hd, pl.ds(r, tm // dil, stride=dil), :].astype(BF16)

    hgrn(range(min(N_GROUPS, n_chunks), n_chunks))
    for t in range(4):
        gab_ref[:, t * COL_TILE:(t + 1) * COL_TILE] = _sigmoid(proj(13 + t)).astype(BF16)


def _inproj(x2, g1, w_bf, lb_raw, rope_c, rope_s, sel, masks, gn, layer, batch, seq, tm):
    t_rows = x2.shape[0]
    s_tiles = seq // tm
    grid = (t_rows // tm,)
    rope_spec = pl.BlockSpec((2, tm, ATT_DIM), lambda i: (0, i % s_tiles, 0))
    in_specs = [
        pl.BlockSpec((tm, D_MODEL), lambda i: (i, 0)),
        pl.BlockSpec((1, D_MODEL), lambda i: (0, 0)),
        pl.BlockSpec((None, D_MODEL, N_IN), lambda i: (layer, 0, 0), pipeline_mode=pl.Buffered(1)),
        pl.BlockSpec(lb_raw.shape, lambda i: (0, 0)),
        rope_spec, rope_spec,
        pl.BlockSpec(sel.shape, lambda i: (0, 0)),
        pl.BlockSpec(masks.shape, lambda i: (0, 0, 0)),
        pl.BlockSpec((1, HG_W), lambda i: (0, 0)),
    ]
    row = lambda w: pl.BlockSpec((tm, w), lambda i: (i, 0))
    att_spec = lambda dil: pl.BlockSpec(
        (None, dil, tm // dil, 3 * ATT_W), lambda i: (i // s_tiles, 0, i % s_tiles, 0))
    out_specs = [row(HG_W)]
    out_specs += [att_spec(dil) for _, dil in ATT_GROUPS]
    out_specs += [row(2 * D_MODEL)]
    out_shape = [jax.ShapeDtypeStruct((t_rows, HG_W), BF16)]
    out_shape += [jax.ShapeDtypeStruct((batch, dil, seq // dil, 3 * ATT_W), BF16)
                  for _, dil in ATT_GROUPS]
    out_shape += [jax.ShapeDtypeStruct((t_rows, 2 * D_MODEL), BF16)]
    return pl.pallas_call(
        functools.partial(_inproj_body, layer, tm, s_tiles),
        grid=grid, in_specs=in_specs, out_specs=out_specs, out_shape=out_shape,
        scratch_shapes=[pltpu.VMEM((2, ATT_HEADS, tm, ATT_DIM), F32),
                        pltpu.VMEM((HG_HEADS, HG_DIM, HG_DIM), F32)],
        compiler_params=pltpu.CompilerParams(
            dimension_semantics=("arbitrary",), vmem_limit_bytes=VMEM_LIMIT),
        name="inproj",
    )(x2, g1[layer], w_bf, lb_raw, rope_c, rope_s, sel, masks, gn[layer])


def _hgrn_tables():
    c = HG_CHUNK
    t = np.arange(c)[:, None]
    u = np.arange(c)[None, :]
    slabs = [u <= t]
    masks = [np.eye(c, dtype=bool)]
    for lvl in range(HG_LEVELS):
        beta = 1 << lvl
        mid = (t // (2 * beta)) * (2 * beta) + beta - 1
        right = t > mid
        if 0 < lvl < HG_SEL_LEVELS:
            slabs.append(np.where(right, (u > mid) & (u <= t), (u > t) & (u <= mid)))
        same = (t // (2 * beta)) == (u // (2 * beta))
        masks.append(same & right & ((u % (2 * beta)) < beta))
    m = np.concatenate(slabs, axis=0).astype(np.float32)
    m2 = np.concatenate([m, m], axis=1)
    return jnp.asarray(m2, BF16), jnp.asarray(np.stack(masks).astype(np.float32))


def _hgrn_rows(chunks, q, lf, k, v, gate, sel_ref, mask_ref, gn_ref, st_ref, o_ref):
    c = HG_CHUNK
    rid = lax.broadcasted_iota(jnp.int32, (c, 1), 0)
    row_right = [((rid >> lvl) & 1) == 1 for lvl in range(HG_SEL_LEVELS)]

    for ci in chunks:
        rows = slice(ci * c, (ci + 1) * c)
        g = lf[rows]
        g_hi = g.astype(BF16)
        g_lo = (g - g_hi.astype(F32)).astype(BF16)
        sums = _dot(sel_ref[...], jnp.concatenate([g_hi, g_lo], axis=0))
        b = sums[0:c]
        e_lvl = [jnp.where(row_right[0], jnp.exp(g), 1.0)]
        e_lvl += [jnp.exp(sums[lvl * c:(lvl + 1) * c]) for lvl in range(1, HG_SEL_LEVELS)]
        for lvl in range(HG_SEL_LEVELS, HG_LEVELS):
            beta = 1 << lvl
            parts = []
            for g0 in range(0, c, 2 * beta):
                b_mid = b[g0 + beta - 1:g0 + beta, :]
                parts.append(b_mid - b[g0:g0 + beta])
                parts.append(b[g0 + beta:g0 + 2 * beta] - b_mid)
            e_lvl.append(jnp.exp(jnp.concatenate(parts, axis=0)))
        e_b = jnp.exp(b)
        e_last = jnp.exp(b[c - 1:c, :] - b)
        for h in range(HG_HEADS):
            hs = slice(h * HG_DIM, (h + 1) * HG_DIM)
            qh = q[rows, hs]
            kh = k[rows, hs]
            vh = v[rows, hs]
            a = mask_ref[0] * _dot_nt(qh.astype(BF16), kh.astype(BF16))
            for lvl in range(HG_SEL_LEVELS):
                x = (jnp.where(row_right[lvl], qh, kh) * e_lvl[lvl][:, hs]).astype(BF16)
                a = a + mask_ref[lvl + 1] * _dot_nt(x, x)
            a_rows = [a[r0:r0 + 8] for r0 in range(0, c, 8)]
            for lvl in range(HG_SEL_LEVELS, HG_LEVELS):
                beta = 1 << lvl
                e = e_lvl[lvl][:, hs]
                right = [r0 for r0 in range(0, c, 8) if (r0 // beta) % 2 == 1]
                ql = jnp.concatenate([qh[r0:r0 + 8] * e[r0:r0 + 8] for r0 in right], axis=0)
                kl = jnp.concatenate(
                    [jnp.zeros((8, HG_DIM), F32) if (r0 // beta) % 2 == 1
                     else kh[r0:r0 + 8] * e[r0:r0 + 8] for r0 in range(0, c, 8)], axis=0)
                p = _dot_nt(ql.astype(BF16), kl.astype(BF16))
                for i, r0 in enumerate(right):
                    pi = p[8 * i:8 * i + 8]
                    if 2 * beta < c:
                        pi = mask_ref[lvl + 1, r0:r0 + 8, :] * pi
                    a_rows[r0 // 8] = a_rows[r0 // 8] + pi
            a = jnp.concatenate(a_rows, axis=0)
            st = st_ref[h]
            o = (_dot(a.astype(BF16), vh)
                 + _dot_nt((qh * e_b[:, hs]).astype(BF16), st.astype(BF16)))
            st_ref[h] = (st * e_b[c - 1:c, hs]
                         + _dot_tn(vh, (kh * e_last[:, hs]).astype(BF16)))
            ms = jnp.mean(o * o, axis=-1, keepdims=True)
            o_ref[rows, hs] = (o * lax.rsqrt(ms + EPS) * gn_ref[:, hs] * gate[rows, hs]).astype(BF16)


def _attn_body(dil, n_blk, res_unroll, q_ref, kc_ref, kp_ref, vc_ref, vp_ref, o_ref, lse_ref, o_scr):
    n = pl.program_id(1)
    blk = ATT_L
    i = lax.broadcasted_iota(jnp.int32, (blk, 2 * blk), 0)
    j = lax.broadcasted_iota(jnp.int32, (blk, 2 * blk), 1)
    band = (j >= i) & (j <= i + blk)
    band_first = band & ((n > 0) | (j >= blk))
    lane = lax.broadcasted_iota(jnp.int32, (blk, ATT_DIM), 1)
    ones = jnp.ones((2 * blk, ATT_DIM), BF16)

    def residue(r, carry):
        for jb in range(n_blk):
            cur = slice(jb * blk, (jb + 1) * blk)
            prev = slice((jb - 1) * blk, jb * blk)
            valid = band_first if jb == 0 else band
            lse_tile = jnp.zeros((blk, ATT_DIM), F32)
            out_rows = pl.ds(jb * blk * dil + r, blk, stride=dil)
            for h in range(ATT_HEADS):
                hs = slice(h * ATT_DIM, (h + 1) * ATT_DIM)
                q = q_ref[r, cur, hs]
                if jb == 0:
                    k_prev, v_prev = kp_ref[r, :, hs], vp_ref[r, :, hs]
                else:
                    k_prev, v_prev = kc_ref[r, prev, hs], vc_ref[r, prev, hs]
                kk = jnp.concatenate([k_prev, kc_ref[r, cur, hs]], axis=0)
                vv = jnp.concatenate([v_prev, vc_ref[r, cur, hs]], axis=0)
                s = jnp.where(valid, _dot_nt(q, kk), NEG_BIG)
                m = jnp.max(s, axis=-1, keepdims=True)
                p = jnp.exp(s - m).astype(BF16)
                ov = _dot(p, jnp.concatenate([vv, ones], axis=1))
                den = ov[:, ATT_DIM:]
                o_scr[h, out_rows, :] = ov[:, :ATT_DIM] / den
                lse_tile = jnp.where(lane == h, m + jnp.log(den), lse_tile)
            lse_ref[out_rows, :] = lse_tile
        return carry

    def residues(it, carry):
        for rr in range(res_unroll):
            residue(it * res_unroll + rr, carry)
        return carry

    if dil == res_unroll:
        residues(0, 0)
    else:
        lax.fori_loop(0, dil // res_unroll, residues, 0)
    for h in range(ATT_HEADS):
        o_ref[:, h * ATT_DIM:(h + 1) * ATT_DIM] = o_scr[h].astype(o_ref.dtype)


def _attn_group(qkv_g, grp, batch, seq, n_blk, res_unroll):
    _, dil = ATT_GROUPS[grp]
    m_rows = seq // dil
    steps = m_rows // (ATT_L * n_blk)
    span = ATT_L * n_blk * dil
    cur = lambda c: pl.BlockSpec((None, dil, ATT_L * n_blk, ATT_W), lambda b, n: (b, 0, n, c))
    prev = lambda c: pl.BlockSpec(
        (None, dil, ATT_L, ATT_W), lambda b, n: (b, 0, jnp.maximum(n * n_blk - 1, 0), c))
    in_specs = [cur(0), cur(1), prev(1), cur(2), prev(2)]
    out_specs = [
        pl.BlockSpec((None, span, ATT_W), lambda b, n: (b, n, 0)),
        pl.BlockSpec((None, span, ATT_DIM), lambda b, n: (b, n, 0)),
    ]
    out_shape = [
        jax.ShapeDtypeStruct((batch, seq, ATT_W), BF16),
        jax.ShapeDtypeStruct((batch, seq, ATT_DIM), F32),
    ]
    o, lse = pl.pallas_call(
        functools.partial(_attn_body, dil, n_blk, res_unroll),
        grid=(batch, steps), in_specs=in_specs, out_specs=out_specs, out_shape=out_shape,
        scratch_shapes=[pltpu.VMEM((ATT_HEADS, span, ATT_DIM), F32)],
        compiler_params=pltpu.CompilerParams(
            dimension_semantics=("arbitrary", "arbitrary"), vmem_limit_bytes=VMEM_LIMIT),
        name=f"attn_g{grp}",
    )(qkv_g, qkv_g, qkv_g, qkv_g, qkv_g)
    return o.reshape(batch * seq, ATT_W), lse.reshape(batch * seq, ATT_DIM)


def _mix_body(final, ff_tile, x_ref, ohg_ref, o0_ref, o1_ref, o2_ref, l0_ref, l1_ref, l2_ref,
              gab_ref, wa_ref, wb_ref, wo_ref, g2_ref, wu_ref, wd_ref, gf_ref, out_ref):
    lses = [l0_ref[...], l1_ref[...], l2_ref[...]]
    mx = jnp.maximum(jnp.maximum(lses[0], lses[1]), lses[2])
    es = [jnp.exp(l - mx) for l in lses]
    inv = 1.0 / (es[0] + es[1] + es[2])
    heads = [None] * ATT_HEADS
    for e, o_ref in zip(es, (o0_ref, o1_ref, o2_ref)):
        w = e * inv
        for h in range(ATT_HEADS):
            w_h = jnp.broadcast_to(w[:, h:h + 1], (w.shape[0], ATT_DIM))
            term = w_h * o_ref[:, h * ATT_DIM:(h + 1) * ATT_DIM].astype(F32)
            heads[h] = term if heads[h] is None else heads[h] + term
    o_att = jnp.concatenate(heads, axis=1)
    ga = gab_ref[:, :D_MODEL].astype(F32)
    gb = gab_ref[:, D_MODEL:].astype(F32)
    y = ga * _dot(ohg_ref[...], wa_ref[...]) + gb * _dot(o_att.astype(BF16), wb_ref[...])
    x = x_ref[...] + _dot(y.astype(BF16), wo_ref[...])

    ms = jnp.mean(x * x, axis=-1, keepdims=True)
    h = (x * lax.rsqrt(ms + EPS) * g2_ref[...]).astype(BF16)
    acc = x
    for c in range(D_FF // ff_tile):
        cs = slice(c * ff_tile, (c + 1) * ff_tile)
        u = jnp.maximum(_dot(h, wu_ref[:, cs]), 0.0)
        acc = acc + _dot((u * u).astype(BF16), wd_ref[cs, :])
    if final:
        ms2 = jnp.mean(acc * acc, axis=-1, keepdims=True)
        acc = acc * lax.rsqrt(ms2 + EPS) * gf_ref[...]
    out_ref[...] = acc


def _mix(x2, ohg, outs, lses, gab, wa, wb, wo, g2, wu, wd, gf, layer, final, tm):
    t_rows = x2.shape[0]
    grid = (t_rows // tm,)
    row = lambda w: pl.BlockSpec((tm, w), lambda i: (i, 0))
    of_layer = lambda a: pl.BlockSpec((None,) + a.shape[1:], lambda i: (layer, 0, 0),
                                      pipeline_mode=pl.Buffered(1))
    vec = pl.BlockSpec((1, D_MODEL), lambda i: (0, 0))
    in_specs = [row(D_MODEL), row(HG_W), row(ATT_W), row(ATT_W), row(ATT_W),
                row(ATT_DIM), row(ATT_DIM), row(ATT_DIM), row(2 * D_MODEL),
                of_layer(wa), of_layer(wb), of_layer(wo), vec,
                of_layer(wu), of_layer(wd), vec]
    return pl.pallas_call(
        functools.partial(_mix_body, final, 1024),
        grid=grid, in_specs=in_specs, out_specs=row(D_MODEL),
        out_shape=jax.ShapeDtypeStruct((t_rows, D_MODEL), F32),
        compiler_params=pltpu.CompilerParams(
            dimension_semantics=("arbitrary",), vmem_limit_bytes=VMEM_LIMIT),
        name="mix",
    )(x2, ohg, outs[0], outs[1], outs[2], lses[0], lses[1], lses[2], gab, wa, wb, wo,
      g2[layer], wu, wd, gf)


def _rope_perm():
    d = np.arange(ATT_DIM)
    half = ATT_DIM // 2
    plain = d[ROPE_DIM:]
    n_first = half - ROPE_HALF
    return np.concatenate([d[:ROPE_HALF], plain[:n_first], d[ROPE_HALF:ROPE_DIM], plain[n_first:]])


def _prep_w_in_body(w_ref, o_ref):
    j = pl.program_id(1)
    is_qk = jnp.logical_and(j >= QK_TILE0, j < VG_TILE0)

    @pl.when(is_qk)
    def _():
        half = ATT_DIM // 2
        for hd in range(ATT_HEADS):
            t = w_ref[:, hd * ATT_DIM:(hd + 1) * ATT_DIM]
            t = jnp.concatenate([t[:, :ROPE_HALF], t[:, ROPE_DIM:ROPE_HALF + half],
                                 t[:, ROPE_HALF:ROPE_DIM], t[:, ROPE_HALF + half:]], axis=1)
            o_ref[:, hd * ATT_DIM:(hd + 1) * ATT_DIM] = t.astype(BF16)

    @pl.when(jnp.logical_not(is_qk))
    def _():
        o_ref[...] = w_ref[...].astype(BF16)


def _prep_w_in(w_in):
    depth = w_in.shape[0]
    spec = pl.BlockSpec((None, D_MODEL, COL_TILE), lambda l, j: (l, 0, j))
    return pl.pallas_call(
        _prep_w_in_body, grid=(depth, N_IN // COL_TILE), in_specs=[spec], out_specs=spec,
        out_shape=jax.ShapeDtypeStruct(w_in.shape, BF16),
        compiler_params=pltpu.CompilerParams(
            dimension_semantics=("arbitrary", "arbitrary"), vmem_limit_bytes=VMEM_LIMIT),
        name="prep_w_in",
    )(w_in)


def _rope_tables(seq):
    pos = jnp.arange(seq, dtype=F32)
    inv_freq = ROPE_THETA ** (-jnp.arange(0, ROPE_DIM, 2, dtype=F32) / ROPE_DIM)
    ang = pos[:, None] * inv_freq[None, :]
    cos, sin = jnp.cos(ang), jnp.sin(ang)
    n_plain = ATT_DIM // 2 - ROPE_HALF
    ones = jnp.ones((seq, n_plain), F32)
    zeros = jnp.zeros((seq, n_plain), F32)
    c = jnp.concatenate([cos, ones, cos, ones], axis=1)
    s = jnp.concatenate([-sin, zeros, sin, zeros], axis=1)
    scale = ATT_DIM ** -0.5
    stack = lambda t: jnp.stack([t * scale, t], axis=0)
    return stack(c), stack(s)


def kernel(x, norm1_g, w_in, hg_lower_bounds, hg_norm_g, w_branch_a, w_branch_b, w_out, norm2_g,
           w_up, w_down, final_norm_g):
    batch, seq, d = x.shape
    depth = w_in.shape[0]
    t_rows = batch * seq
    tm = 512
    tm_in = 512
    rope_c, rope_s = _rope_tables(seq)
    sel, masks = _hgrn_tables()
    lb_raw = hg_lower_bounds.astype(F32)
    gf = final_norm_g.reshape(1, d).astype(F32)
    w_bf = _prep_w_in(w_in)
    wa, wb, wo = w_branch_a.astype(BF16), w_branch_b.astype(BF16), w_out.astype(BF16)
    wu, wd = w_up.astype(BF16), w_down.astype(BF16)
    g1 = norm1_g.reshape(depth, 1, d)
    g2 = norm2_g.reshape(depth, 1, d)
    gn = hg_norm_g.reshape(depth, 1, HG_W)

    x2 = x.reshape(t_rows, d)
    for l in range(depth):
        ohg, a0, a1, a2, gab = _inproj(x2, g1, w_bf, lb_raw, rope_c, rope_s, sel, masks, gn,
                                       l, batch, seq, tm_in)
        outs, lses = [], []
        for grp, qkv_g in enumerate((a0, a1, a2)):
            o_g, lse_g = _attn_group(qkv_g, grp, batch, seq, ATT_BLOCKS_PER_STEP[grp],
                                     ATT_RESIDUE_UNROLL[grp])
            outs.append(o_g)
            lses.append(lse_g)
        x2 = _mix(x2, ohg, outs, lses, gab, wa, wb, wo, g2, wu, wd, gf,
                  l, l == depth - 1, tm)
    return x2.reshape(batch, seq, d)
```

```python
import functools

import numpy as np
import jax
import jax.numpy as jnp
from jax import lax
from jax.experimental import pallas as pl
from jax.experimental.pallas import tpu as pltpu

F32 = jnp.float32
BF16 = jnp.bfloat16

D_MODEL = 1024
HG_HEADS = 4
HG_DIM = 128
HG_W = HG_HEADS * HG_DIM
ATT_GROUPS = ((128, 1), (512, 4), (2048, 16))
N_GROUPS = len(ATT_GROUPS)
ATT_HEADS = 4
ATT_DIM = 128
ATT_W = ATT_HEADS * ATT_DIM
ATT_QKV_W = 3 * N_GROUPS * ATT_W
ATT_L = 128
ROPE_THETA = 500000.0
ROPE_DIM = ATT_DIM // 4
ROPE_HALF = ROPE_DIM // 2
D_FF = 4 * D_MODEL
EPS = 1e-6
N_IN = 4 * HG_W + ATT_QKV_W + 2 * D_MODEL

COL_TILE = 512
QK_TILE0 = 4 * HG_W // COL_TILE
VG_TILE0 = QK_TILE0 + 2 * N_GROUPS * ATT_W // COL_TILE
HG_CHUNK = 128
HG_LEVELS = 7
HG_SEL_LEVELS = 3
NEG_BIG = -1e30
ATT_BLOCKS_PER_STEP = (16, 4, 1)
ATT_RESIDUE_UNROLL = (1, 4, 16)
VMEM_LIMIT = 56 * 1024 * 1024


def _sigmoid(z):
    return 0.5 * jnp.tanh(0.5 * z) + 0.5


def _dot(a, b):
    return jnp.dot(a, b, preferred_element_type=F32)


def _dot_nt(a, b):
    return lax.dot_general(a, b, (((1,), (1,)), ((), ())), preferred_element_type=F32)


def _dot_tn(a, b):
    return lax.dot_general(a, b, (((0,), (0,)), ((), ())), preferred_element_type=F32)


def _inproj_body(layer, tm, s_tiles, x_ref, g1_ref, w_ref, lb_ref,
                 rc_ref, rs_ref, sel_ref, mask_ref, gn_ref,
                 ohg_ref, a0_ref, a1_ref, a2_ref, gab_ref, scr_ref, st_ref):
    @pl.when(pl.program_id(0) % s_tiles == 0)
    def _():
        st_ref[...] = jnp.zeros_like(st_ref)

    half = tm // 2
    x = jnp.concatenate([x_ref[0], x_ref[1]], axis=0)
    ms = jnp.mean(x * x, axis=-1, keepdims=True)
    h = (x * lax.rsqrt(ms + EPS) * g1_ref[...]).astype(BF16)

    def proj(tile):
        return _dot(h, w_ref[:, tile * COL_TILE:(tile + 1) * COL_TILE])

    acc = proj(0)
    q_hg = acc * _sigmoid(acc)

    acc = proj(1)
    raw = lb_ref[...]
    rows = [raw[r:r + 1, :] for r in range(raw.shape[0])]
    mx = functools.reduce(jnp.maximum, rows)
    ex = [jnp.exp(r - mx) for r in rows]
    den = functools.reduce(lambda a, b: a + b, ex)
    sm = [e / den for e in ex]
    lb = functools.reduce(lambda a, b: a + b, sm[:layer + 1]) - sm[0]
    log_sig = jnp.minimum(acc, 0.0) - jnp.log1p(jnp.exp(-jnp.abs(acc)))
    a = jnp.log(lb)
    c = jnp.log1p(-lb) + log_sig
    log_f = jnp.maximum(a, c) + jnp.log1p(jnp.exp(-jnp.abs(a - c)))
    k_hg = (1.0 - lb) / (1.0 + jnp.exp(acc))
    v_hg = proj(2).astype(BF16)
    gate_hg = _sigmoid(proj(3))

    def hgrn(chunks):
        _hgrn_rows(chunks, q_hg, log_f, k_hg, v_hg, gate_hg,
                   sel_ref, mask_ref, gn_ref, st_ref, ohg_ref)

    n_chunks = tm // HG_CHUNK
    per_seq = n_chunks // 2
    order = [sq * per_seq + i for i in range(per_seq) for sq in range(2)]
    att_refs = (a0_ref, a1_ref, a2_ref)
    n_split = 0
    for grp in range(N_GROUPS):
        dil = ATT_GROUPS[grp][1]
        hgrn(order[grp:grp + 1])
        for which in range(3):
            acc = proj(4 + which * N_GROUPS + grp)
            pieces = [acc[:, hd * ATT_DIM:(hd + 1) * ATT_DIM] for hd in range(ATT_HEADS)]
            if which < 2:
                pieces = [t * rc_ref[which] + pltpu.roll(t, ATT_DIM // 2, 1) * rs_ref[which]
                          for t in pieces]
            if dil == 1:
                for hd in range(ATT_HEADS):
                    c0 = which * ATT_W + hd * ATT_DIM
                    for sq in range(2):
                        att_refs[grp][sq, 0, :, c0:c0 + ATT_DIM] = pieces[hd][
                            sq * half:(sq + 1) * half].astype(BF16)
            else:
                slot = n_split % scr_ref.shape[0]
                n_split += 1
                for hd in range(ATT_HEADS):
                    scr_ref[slot, hd] = pieces[hd]
                for r in range(dil):
                    for hd in range(ATT_HEADS):
                        c0 = which * ATT_W + hd * ATT_DIM
                        for sq in range(2):
                            att_refs[grp][sq, r, :, c0:c0 + ATT_DIM] = scr_ref[
                                slot, hd, pl.ds(sq * half + r, half // dil, stride=dil), :].astype(BF16)

    hgrn(order[N_GROUPS:])
    for t in range(4):
        gates = _sigmoid(proj(13 + t)).astype(BF16)
        for sq in range(2):
            gab_ref[sq, :, t * COL_TILE:(t + 1) * COL_TILE] = gates[sq * half:(sq + 1) * half]


def _inproj(x2, g1, w_bf, lb_raw, rope_c, rope_s, sel, masks, gn, layer, batch, seq, tm):
    t_rows = x2.shape[0]
    half = tm // 2
    s_tiles = seq // half
    pairs = batch // 2
    grid = (pairs * s_tiles,)
    x2 = x2.reshape(2, pairs, seq, D_MODEL)
    rope_spec = pl.BlockSpec((2, tm, ATT_DIM), lambda i: (0, i % s_tiles, 0))
    in_specs = [
        pl.BlockSpec((2, None, half, D_MODEL), lambda i: (0, i // s_tiles, i % s_tiles, 0)),
        pl.BlockSpec((1, D_MODEL), lambda i: (0, 0)),
        pl.BlockSpec((None, D_MODEL, N_IN), lambda i: (layer, 0, 0), pipeline_mode=pl.Buffered(1)),
        pl.BlockSpec(lb_raw.shape, lambda i: (0, 0)),
        rope_spec, rope_spec,
        pl.BlockSpec(sel.shape, lambda i: (0, 0)),
        pl.BlockSpec(masks.shape, lambda i: (0, 0, 0)),
        pl.BlockSpec((1, HG_W), lambda i: (0, 0)),
    ]
    row = lambda w: pl.BlockSpec((2, None, half, w), lambda i: (0, i // s_tiles, i % s_tiles, 0))
    att_spec = lambda dil: pl.BlockSpec(
        (2, None, dil, half // dil, 3 * ATT_W), lambda i: (0, i // s_tiles, 0, i % s_tiles, 0))
    out_specs = [row(HG_W)]
    out_specs += [att_spec(dil) for _, dil in ATT_GROUPS]
    out_specs += [row(2 * D_MODEL)]
    out_shape = [jax.ShapeDtypeStruct((2, pairs, seq, HG_W), BF16)]
    out_shape += [jax.ShapeDtypeStruct((2, pairs, dil, seq // dil, 3 * ATT_W), BF16)
                  for _, dil in ATT_GROUPS]
    out_shape += [jax.ShapeDtypeStruct((2, pairs, seq, 2 * D_MODEL), BF16)]
    ohg, a0, a1, a2, gab = pl.pallas_call(
        functools.partial(_inproj_body, layer, tm, s_tiles),
        grid=grid, in_specs=in_specs, out_specs=out_specs, out_shape=out_shape,
        scratch_shapes=[pltpu.VMEM((2, ATT_HEADS, tm, ATT_DIM), F32),
                        pltpu.VMEM((2 * HG_HEADS, HG_DIM, HG_DIM), F32)],
        compiler_params=pltpu.CompilerParams(
            dimension_semantics=("arbitrary",), vmem_limit_bytes=VMEM_LIMIT),
        name="inproj",
    )(x2, g1[layer], w_bf, lb_raw, rope_c, rope_s, sel, masks, gn[layer])
    unpair = lambda a: a.reshape((batch,) + a.shape[2:])
    return (ohg.reshape(t_rows, HG_W), unpair(a0), unpair(a1), unpair(a2),
            gab.reshape(t_rows, 2 * D_MODEL))


def _hgrn_tables():
    c = HG_CHUNK
    t = np.arange(c)[:, None]
    u = np.arange(c)[None, :]
    slabs = [u <= t]
    masks = [np.eye(c, dtype=bool)]
    for lvl in range(HG_LEVELS):
        beta = 1 << lvl
        mid = (t // (2 * beta)) * (2 * beta) + beta - 1
        right = t > mid
        if 0 < lvl < HG_SEL_LEVELS:
            slabs.append(np.where(right, (u > mid) & (u <= t), (u > t) & (u <= mid)))
        same = (t // (2 * beta)) == (u // (2 * beta))
        masks.append(same & right & ((u % (2 * beta)) < beta))
    m = np.concatenate(slabs, axis=0).astype(np.float32)
    m2 = np.concatenate([m, m], axis=1)
    return jnp.asarray(m2, BF16), jnp.asarray(np.stack(masks).astype(np.float32))


def _hgrn_rows(chunks, q, lf, k, v, gate, sel_ref, mask_ref, gn_ref, st_ref, o_ref):
    c = HG_CHUNK
    rid = lax.broadcasted_iota(jnp.int32, (c, 1), 0)
    row_right = [((rid >> lvl) & 1) == 1 for lvl in range(HG_SEL_LEVELS)]

    for ci in chunks:
        rows = slice(ci * c, (ci + 1) * c)
        g = lf[rows]
        g_hi = g.astype(BF16)
        g_lo = (g - g_hi.astype(F32)).astype(BF16)
        sums = _dot(sel_ref[...], jnp.concatenate([g_hi, g_lo], axis=0))
        b = sums[0:c]
        e_lvl = [jnp.where(row_right[0], jnp.exp(g), 1.0)]
        e_lvl += [jnp.exp(sums[lvl * c:(lvl + 1) * c]) for lvl in range(1, HG_SEL_LEVELS)]
        for lvl in range(HG_SEL_LEVELS, HG_LEVELS):
            beta = 1 << lvl
            parts = []
            for g0 in range(0, c, 2 * beta):
                b_mid = b[g0 + beta - 1:g0 + beta, :]
                parts.append(b_mid - b[g0:g0 + beta])
                parts.append(b[g0 + beta:g0 + 2 * beta] - b_mid)
            e_lvl.append(jnp.exp(jnp.concatenate(parts, axis=0)))
        e_b = jnp.exp(b)
        e_last = jnp.exp(b[c - 1:c, :] - b)
        for h in range(HG_HEADS):
            hs = slice(h * HG_DIM, (h + 1) * HG_DIM)
            qh = q[rows, hs]
            kh = k[rows, hs]
            vh = v[rows, hs]
            a = mask_ref[0] * _dot_nt(qh.astype(BF16), kh.astype(BF16))
            for lvl in range(HG_SEL_LEVELS):
                x = (jnp.where(row_right[lvl], qh, kh) * e_lvl[lvl][:, hs]).astype(BF16)
                a = a + mask_ref[lvl + 1] * _dot_nt(x, x)
            a_rows = [a[r0:r0 + 8] for r0 in range(0, c, 8)]
            for lvl in range(HG_SEL_LEVELS, HG_LEVELS):
                beta = 1 << lvl
                e = e_lvl[lvl][:, hs]
                right = [r0 for r0 in range(0, c, 8) if (r0 // beta) % 2 == 1]
                ql = jnp.concatenate([qh[r0:r0 + 8] * e[r0:r0 + 8] for r0 in right], axis=0)
                kl = jnp.concatenate(
                    [jnp.zeros((8, HG_DIM), F32) if (r0 // beta) % 2 == 1
                     else kh[r0:r0 + 8] * e[r0:r0 + 8] for r0 in range(0, c, 8)], axis=0)
                p = _dot_nt(ql.astype(BF16), kl.astype(BF16))
                for i, r0 in enumerate(right):
                    pi = p[8 * i:8 * i + 8]
                    if 2 * beta < c:
                        pi = mask_ref[lvl + 1, r0:r0 + 8, :] * pi
                    a_rows[r0 // 8] = a_rows[r0 // 8] + pi
            a = jnp.concatenate(a_rows, axis=0)
            sq = ci // (q.shape[0] // c // 2)
            st = st_ref[sq * HG_HEADS + h]
            o = (_dot(a.astype(BF16), vh)
                 + _dot_nt((qh * e_b[:, hs]).astype(BF16), st.astype(BF16)))
            st_ref[sq * HG_HEADS + h] = (st * e_b[c - 1:c, hs]
                         + _dot_tn(vh, (kh * e_last[:, hs]).astype(BF16)))
            ms = jnp.mean(o * o, axis=-1, keepdims=True)
            r_in = (ci % (q.shape[0] // c // 2)) * c
            o_ref[sq, r_in:r_in + c, hs] = (
                o * lax.rsqrt(ms + EPS) * gn_ref[:, hs] * gate[rows, hs]).astype(BF16)


def _attn_body(dil, n_blk, res_unroll, q_ref, kc_ref, kp_ref, vc_ref, vp_ref, o_ref, lse_ref, o_scr):
    n = pl.program_id(1)
    blk = ATT_L
    i = lax.broadcasted_iota(jnp.int32, (blk, 2 * blk), 0)
    j = lax.broadcasted_iota(jnp.int32, (blk, 2 * blk), 1)
    band = (j >= i) & (j <= i + blk)
    band_first = band & ((n > 0) | (j >= blk))
    lane = lax.broadcasted_iota(jnp.int32, (blk, ATT_DIM), 1)
    ones = jnp.ones((2 * blk, ATT_DIM), BF16)

    def residue(r, carry):
        for jb in range(n_blk):
            cur = slice(jb * blk, (jb + 1) * blk)
            prev = slice((jb - 1) * blk, jb * blk)
            valid = band_first if jb == 0 else band
            lse_tile = jnp.zeros((blk, ATT_DIM), F32)
            out_rows = pl.ds(jb * blk * dil + r, blk, stride=dil)
            for h in range(ATT_HEADS):
                hs = slice(h * ATT_DIM, (h + 1) * ATT_DIM)
                q = q_ref[r, cur, hs]
                if jb == 0:
                    k_prev, v_prev = kp_ref[r, :, hs], vp_ref[r, :, hs]
                else:
                    k_prev, v_prev = kc_ref[r, prev, hs], vc_ref[r, prev, hs]
                kk = jnp.concatenate([k_prev, kc_ref[r, cur, hs]], axis=0)
                vv = jnp.concatenate([v_prev, vc_ref[r, cur, hs]], axis=0)
                s = jnp.where(valid, _dot_nt(q, kk), NEG_BIG)
                m = jnp.max(s, axis=-1, keepdims=True)
                p = jnp.exp(s - m).astype(BF16)
                ov = _dot(p, jnp.concatenate([vv, ones], axis=1))
                den = ov[:, ATT_DIM:]
                o_scr[h, out_rows, :] = ov[:, :ATT_DIM] / den
                lse_tile = jnp.where(lane == h, m + jnp.log(den), lse_tile)
            lse_ref[out_rows, :] = lse_tile
        return carry

    def residues(it, carry):
        for rr in range(res_unroll):
            residue(it * res_unroll + rr, carry)
        return carry

    if dil == res_unroll:
        residues(0, 0)
    else:
        lax.fori_loop(0, dil // res_unroll, residues, 0)
    for h in range(ATT_HEADS):
        o_ref[:, h * ATT_DIM:(h + 1) * ATT_DIM] = o_scr[h].astype(o_ref.dtype)


def _attn_group(qkv_g, grp, batch, seq, n_blk, res_unroll):
    _, dil = ATT_GROUPS[grp]
    m_rows = seq // dil
    steps = m_rows // (ATT_L * n_blk)
    span = ATT_L * n_blk * dil
    cur = lambda c: pl.BlockSpec((None, dil, ATT_L * n_blk, ATT_W), lambda b, n: (b, 0, n, c))
    prev = lambda c: pl.BlockSpec(
        (None, dil, ATT_L, ATT_W), lambda b, n: (b, 0, jnp.maximum(n * n_blk - 1, 0), c))
    in_specs = [cur(0), cur(1), prev(1), cur(2), prev(2)]
    out_specs = [
        pl.BlockSpec((None, span, ATT_W), lambda b, n: (b, n, 0)),
        pl.BlockSpec((None, span, ATT_DIM), lambda b, n: (b, n, 0)),
    ]
    out_shape = [
        jax.ShapeDtypeStruct((batch, seq, ATT_W), BF16),
        jax.ShapeDtypeStruct((batch, seq, ATT_DIM), F32),
    ]
    o, lse = pl.pallas_call(
        functools.partial(_attn_body, dil, n_blk, res_unroll),
        grid=(batch, steps), in_specs=in_specs, out_specs=out_specs, out_shape=out_shape,
        scratch_shapes=[pltpu.VMEM((ATT_HEADS, span, ATT_DIM), F32)],
        compiler_params=pltpu.CompilerParams(
            dimension_semantics=("arbitrary", "arbitrary"), vmem_limit_bytes=VMEM_LIMIT),
        name=f"attn_g{grp}",
    )(qkv_g, qkv_g, qkv_g, qkv_g, qkv_g)
    return o.reshape(batch * seq, ATT_W), lse.reshape(batch * seq, ATT_DIM)


def _mix_body(final, ff_tile, x_ref, ohg_ref, o0_ref, o1_ref, o2_ref, l0_ref, l1_ref, l2_ref,
              gab_ref, wa_ref, wb_ref, wo_ref, g2_ref, wu_ref, wd_ref, gf_ref, out_ref):
    lses = [l0_ref[...], l1_ref[...], l2_ref[...]]
    mx = jnp.maximum(jnp.maximum(lses[0], lses[1]), lses[2])
    es = [jnp.exp(l - mx) for l in lses]
    inv = 1.0 / (es[0] + es[1] + es[2])
    heads = [None] * ATT_HEADS
    for e, o_ref in zip(es, (o0_ref, o1_ref, o2_ref)):
        w = e * inv
        for h in range(ATT_HEADS):
            w_h = jnp.broadcast_to(w[:, h:h + 1], (w.shape[0], ATT_DIM))
            term = w_h * o_ref[:, h * ATT_DIM:(h + 1) * ATT_DIM].astype(F32)
            heads[h] = term if heads[h] is None else heads[h] + term
    o_att = jnp.concatenate(heads, axis=1)
    ga = gab_ref[:, :D_MODEL].astype(F32)
    gb = gab_ref[:, D_MODEL:].astype(F32)
    y = ga * _dot(ohg_ref[...], wa_ref[...]) + gb * _dot(o_att.astype(BF16), wb_ref[...])
    x = x_ref[...] + _dot(y.astype(BF16), wo_ref[...])

    ms = jnp.mean(x * x, axis=-1, keepdims=True)
    h = (x * lax.rsqrt(ms + EPS) * g2_ref[...]).astype(BF16)
    acc = x
    for c in range(D_FF // ff_tile):
        cs = slice(c * ff_tile, (c + 1) * ff_tile)
        u = jnp.maximum(_dot(h, wu_ref[:, cs]), 0.0)
        acc = acc + _dot((u * u).astype(BF16), wd_ref[cs, :])
    if final:
        ms2 = jnp.mean(acc * acc, axis=-1, keepdims=True)
        acc = acc * lax.rsqrt(ms2 + EPS) * gf_ref[...]
    out_ref[...] = acc


def _mix(x2, ohg, outs, lses, gab, wa, wb, wo, g2, wu, wd, gf, layer, final, tm):
    t_rows = x2.shape[0]
    grid = (t_rows // tm,)
    row = lambda w: pl.BlockSpec((tm, w), lambda i: (i, 0))
    of_layer = lambda a: pl.BlockSpec((None,) + a.shape[1:], lambda i: (layer, 0, 0),
                                      pipeline_mode=pl.Buffered(1))
    vec = pl.BlockSpec((1, D_MODEL), lambda i: (0, 0))
    in_specs = [row(D_MODEL), row(HG_W), row(ATT_W), row(ATT_W), row(ATT_W),
                row(ATT_DIM), row(ATT_DIM), row(ATT_DIM), row(2 * D_MODEL),
                of_layer(wa), of_layer(wb), of_layer(wo), vec,
                of_layer(wu), of_layer(wd), vec]
    return pl.pallas_call(
        functools.partial(_mix_body, final, 1024),
        grid=grid, in_specs=in_specs, out_specs=row(D_MODEL),
        out_shape=jax.ShapeDtypeStruct((t_rows, D_MODEL), F32),
        compiler_params=pltpu.CompilerParams(
            dimension_semantics=("arbitrary",), vmem_limit_bytes=VMEM_LIMIT),
        name="mix",
    )(x2, ohg, outs[0], outs[1], outs[2], lses[0], lses[1], lses[2], gab, wa, wb, wo,
      g2[layer], wu, wd, gf)


def _rope_perm():
    d = np.arange(ATT_DIM)
    half = ATT_DIM // 2
    plain = d[ROPE_DIM:]
    n_first = half - ROPE_HALF
    return np.concatenate([d[:ROPE_HALF], plain[:n_first], d[ROPE_HALF:ROPE_DIM], plain[n_first:]])


def _prep_w_in_body(w_ref, o_ref):
    j = pl.program_id(1)
    is_qk = jnp.logical_and(j >= QK_TILE0, j < VG_TILE0)

    @pl.when(is_qk)
    def _():
        half = ATT_DIM // 2
        for hd in range(ATT_HEADS):
            t = w_ref[:, hd * ATT_DIM:(hd + 1) * ATT_DIM]
            t = jnp.concatenate([t[:, :ROPE_HALF], t[:, ROPE_DIM:ROPE_HALF + half],
                                 t[:, ROPE_HALF:ROPE_DIM], t[:, ROPE_HALF + half:]], axis=1)
            o_ref[:, hd * ATT_DIM:(hd + 1) * ATT_DIM] = t.astype(BF16)

    @pl.when(jnp.logical_not(is_qk))
    def _():
        o_ref[...] = w_ref[...].astype(BF16)


def _prep_w_in(w_in):
    depth = w_in.shape[0]
    spec = pl.BlockSpec((None, D_MODEL, COL_TILE), lambda l, j: (l, 0, j))
    return pl.pallas_call(
        _prep_w_in_body, grid=(depth, N_IN // COL_TILE), in_specs=[spec], out_specs=spec,
        out_shape=jax.ShapeDtypeStruct(w_in.shape, BF16),
        compiler_params=pltpu.CompilerParams(
            dimension_semantics=("arbitrary", "arbitrary"), vmem_limit_bytes=VMEM_LIMIT),
        name="prep_w_in",
    )(w_in)


def _rope_tables(seq):
    pos = jnp.arange(seq, dtype=F32)
    inv_freq = ROPE_THETA ** (-jnp.arange(0, ROPE_DIM, 2, dtype=F32) / ROPE_DIM)
    ang = pos[:, None] * inv_freq[None, :]
    cos, sin = jnp.cos(ang), jnp.sin(ang)
    n_plain = ATT_DIM // 2 - ROPE_HALF
    ones = jnp.ones((seq, n_plain), F32)
    zeros = jnp.zeros((seq, n_plain), F32)
    c = jnp.concatenate([cos, ones, cos, ones], axis=1)
    s = jnp.concatenate([-sin, zeros, sin, zeros], axis=1)
    scale = ATT_DIM ** -0.5
    stack = lambda t: jnp.stack([t * scale, t], axis=0)
    return stack(c), stack(s)


def _pair_rows(tbl, half):
    two, seq, w = tbl.shape
    t = tbl.reshape(two, seq // half, 1, half, w)
    return jnp.broadcast_to(t, (two, seq // half, 2, half, w)).reshape(two, 2 * seq, w)


def kernel(x, norm1_g, w_in, hg_lower_bounds, hg_norm_g, w_branch_a, w_branch_b, w_out, norm2_g,
           w_up, w_down, final_norm_g):
    batch, seq, d = x.shape
    depth = w_in.shape[0]
    t_rows = batch * seq
    tm = 512
    tm_in = 512
    rope_c, rope_s = (_pair_rows(t, tm_in // 2) for t in _rope_tables(seq))
    sel, masks = _hgrn_tables()
    lb_raw = hg_lower_bounds.astype(F32)
    gf = final_norm_g.reshape(1, d).astype(F32)
    w_bf = _prep_w_in(w_in)
    wa, wb, wo = w_branch_a.astype(BF16), w_branch_b.astype(BF16), w_out.astype(BF16)
    wu, wd = w_up.astype(BF16), w_down.astype(BF16)
    g1 = norm1_g.reshape(depth, 1, d)
    g2 = norm2_g.reshape(depth, 1, d)
    gn = hg_norm_g.reshape(depth, 1, HG_W)

    x2 = x.reshape(t_rows, d)
    for l in range(depth):
        ohg, a0, a1, a2, gab = _inproj(x2, g1, w_bf, lb_raw, rope_c, rope_s, sel, masks, gn,
                                       l, batch, seq, tm_in)
        outs, lses = [], []
        for grp, qkv_g in enumerate((a0, a1, a2)):
            o_g, lse_g = _attn_group(qkv_g, grp, batch, seq, ATT_BLOCKS_PER_STEP[grp],
                                     ATT_RESIDUE_UNROLL[grp])
            outs.append(o_g)
            lses.append(lse_g)
        x2 = _mix(x2, ohg, outs, lses, gab, wa, wb, wo, g2, wu, wd, gf,
                  l, l == depth - 1, tm)
    return x2.reshape(batch, seq, d)
```
